```python
import math
import jax, jax.numpy as jnp
from jax import lax
import numpy as np

D_MODEL = 4096
BATCH = 2
SEQ = 4096
DEPTH = 2

CHUNK = 64
Q_BLOCK = 128
EPS = 1e-6
N_BRANCH = 3
BRANCH_W = 2048

RET_HEADS = 8
RET_DK = 256
RET_DV = BRANCH_W // RET_HEADS
ROPE_BASE = 10000.0

GDN_HEADS = 16
GDN_DK = 128
GDN_DV = BRANCH_W // GDN_HEADS
CONV_W = 4
GDN_CONV_CH = 2 * GDN_HEADS * GDN_DK + GDN_HEADS * GDN_DV

DIFF_HEADS = 8
DIFF_DK = 128
DIFF_DV = BRANCH_W // DIFF_HEADS

NUM_BUCKETS = 32
MAX_DISTANCE = 128

IN_SIZES = (
    RET_HEADS * RET_DK, RET_HEADS * RET_DK, BRANCH_W, BRANCH_W,
    GDN_HEADS * GDN_DK, GDN_HEADS * GDN_DK, BRANCH_W, BRANCH_W, GDN_HEADS, GDN_HEADS,
    DIFF_HEADS * 2 * DIFF_DK, DIFF_HEADS * 2 * DIFF_DK, BRANCH_W, BRANCH_W,
    N_BRANCH * D_MODEL,
)
IN_COLS = sum(IN_SIZES)

kernel_name = 'hybrid_retention_gdn_diffattn_gated_block'


def rmsnorm(x, gain):
    xf = x.astype(jnp.float32)
    y = xf * lax.rsqrt(jnp.mean(xf * xf, axis=-1, keepdims=True) + EPS)
    return (y * gain.astype(jnp.float32)).astype(x.dtype)


def l2norm(x):
    xf = x.astype(jnp.float32)
    return xf * lax.rsqrt(jnp.sum(xf * xf, axis=-1, keepdims=True) + EPS)


def rotary(x, pos):
    half = x.shape[-1] // 2
    inv = ROPE_BASE ** (-jnp.arange(half, dtype=jnp.float32) / half)
    ang = pos.astype(jnp.float32)[:, None] * inv[None, :]
    cos = jnp.cos(ang)[None, :, None, :]
    sin = jnp.sin(ang)[None, :, None, :]
    x1, x2 = x[..., :half], x[..., half:]
    return jnp.concatenate([x1 * cos - x2 * sin, x1 * sin + x2 * cos], axis=-1)


def to_chunks(x):
    b, s, h, d = x.shape
    return x.reshape(b, s // CHUNK, CHUNK, h, d).transpose(1, 0, 3, 2, 4)


def from_chunks(x):
    nc, b, h, c, d = x.shape
    return x.transpose(1, 0, 3, 2, 4).reshape(b, nc * c, h, d)


def retention_branch(q, k, v, z, gn_gain):
    b, s, _ = q.shape
    pos = jnp.arange(s)
    qf = rotary(q.astype(jnp.float32).reshape(b, s, RET_HEADS, RET_DK), pos)
    kf = rotary(k.astype(jnp.float32).reshape(b, s, RET_HEADS, RET_DK), pos) * (RET_DK ** -0.5)
    vf = v.astype(jnp.float32).reshape(b, s, RET_HEADS, RET_DV)
    qc, kc, vc = to_chunks(qf), to_chunks(kf), to_chunks(vf)
    log_gamma = jnp.log(1.0 - 2.0 ** (-5.0 - jnp.arange(RET_HEADS, dtype=jnp.float32)))
    idx = jnp.arange(CHUNK, dtype=jnp.float32)
    rel = idx[:, None] - idx[None, :]
    decay = jnp.where(rel >= 0, jnp.exp(jnp.maximum(rel, 0.0)[None] * log_gamma[:, None, None]), 0.0)
    scores = jnp.einsum('nbhqd,nbhkd->nbhqk', qc, kc) * decay
    intra = jnp.einsum('nbhqk,nbhkv->nbhqv', scores, vc)
    q_decay = jnp.exp((idx + 1.0)[None, :] * log_gamma[:, None])
    k_decay = jnp.exp((CHUNK - 1.0 - idx)[None, :] * log_gamma[:, None])
    chunk_decay = jnp.exp(CHUNK * log_gamma)

    def step(state, inp):
        q_i, k_i, v_i = inp
        inter = jnp.einsum('bhqd,bhdv->bhqv', q_i, state) * q_decay[:, :, None]
        state = state * chunk_decay[:, None, None] + jnp.einsum('bhkd,bhkv->bhdv', k_i * k_decay[:, :, None], v_i)
        return state, inter

    state0 = jnp.zeros((b, RET_HEADS, RET_DK, RET_DV), jnp.float32)
    _, inter = lax.scan(step, state0, (qc, kc, vc))
    o = from_chunks(intra + inter)
    mu = jnp.mean(o, axis=-1, keepdims=True)
    var = jnp.mean(jnp.square(o - mu), axis=-1, keepdims=True)
    o = ((o - mu) * lax.rsqrt(var + EPS)).reshape(b, s, BRANCH_W) * gn_gain.astype(jnp.float32)
    return (o * jax.nn.silu(z.astype(jnp.float32))).astype(q.dtype)


def causal_conv(x, w):
    return lax.conv_general_dilated(
        x, w[:, None, :].astype(x.dtype), window_strides=(1,), padding=[(CONV_W - 1, 0)],
        dimension_numbers=('NWC', 'WIO', 'NWC'), feature_group_count=x.shape[-1])


def gdn_branch(q, k, v, z, a, b_logit, conv_w, a_log, dt_bias, norm_gain):
    bsz, s, _ = q.shape
    qkv = jax.nn.silu(causal_conv(jnp.concatenate([q, k, v], axis=-1), conv_w))
    q, k, v = jnp.split(qkv, [GDN_HEADS * GDN_DK, 2 * GDN_HEADS * GDN_DK], axis=-1)
    qf = l2norm(q.reshape(bsz, s, GDN_HEADS, GDN_DK)) * (GDN_DK ** -0.5)
    kf = l2norm(k.reshape(bsz, s, GDN_HEADS, GDN_DK))
    vf = v.astype(jnp.float32).reshape(bsz, s, GDN_HEADS, GDN_DV)
    beta = jax.nn.sigmoid(b_logit.astype(jnp.float32))
    g = -jnp.exp(a_log.astype(jnp.float32)) * jax.nn.softplus(a.astype(jnp.float32) + dt_bias.astype(jnp.float32))
    nc = s // CHUNK
    qc, kc, vc = to_chunks(qf), to_chunks(kf), to_chunks(vf)
    betac = beta.reshape(bsz, nc, CHUNK, GDN_HEADS).transpose(1, 0, 3, 2)
    gc = jnp.cumsum(g.reshape(bsz, nc, CHUNK, GDN_HEADS).transpose(1, 0, 3, 2), axis=-1)
    idx = jnp.arange(CHUNK)
    tril_incl = idx[:, None] >= idx[None, :]
    strict = idx[:, None] > idx[None, :]
    decay = jnp.exp(jnp.where(tril_incl, gc[..., :, None] - gc[..., None, :], -jnp.inf))
    k_beta = kc * betac[..., None]
    v_beta = vc * betac[..., None]
    a_mat = jnp.where(strict, jnp.einsum('nbhid,nbhjd->nbhij', k_beta, kc) * decay, 0.0)
    lower = a_mat + jnp.eye(CHUNK, dtype=jnp.float32)
    rhs = jnp.concatenate([v_beta, k_beta * jnp.exp(gc)[..., None]], axis=-1)
    sol = lax.linalg.triangular_solve(lower, rhs, left_side=True, lower=True, unit_diagonal=True)
    u, w = sol[..., :GDN_DV], sol[..., GDN_DV:]
    attn = jnp.einsum('nbhid,nbhjd->nbhij', qc, kc) * decay
    q_g = qc * jnp.exp(gc)[..., None]
    g_last = gc[..., -1]
    k_g = kc * jnp.exp(g_last[..., None] - gc)[..., None]

    def step(state, inp):
        u_i, w_i, attn_i, q_i, k_i, gl = inp
        v_new = u_i - jnp.einsum('bhck,bhkv->bhcv', w_i, state)
        o = jnp.einsum('bhck,bhkv->bhcv', q_i, state) + jnp.einsum('bhij,bhjv->bhiv', attn_i, v_new)
        state = state * jnp.exp(gl)[..., None, None] + jnp.einsum('bhck,bhcv->bhkv', k_i, v_new)
        return state, o

    state0 = jnp.zeros((bsz, GDN_HEADS, GDN_DK, GDN_DV), jnp.float32)
    _, o = lax.scan(step, state0, (u, w, attn, q_g, k_g, g_last))
    o = from_chunks(o)
    o = o * lax.rsqrt(jnp.mean(o * o, axis=-1, keepdims=True) + EPS) * norm_gain.astype(jnp.float32)
    o = o.reshape(bsz, s, BRANCH_W) * jax.nn.silu(z.astype(jnp.float32))
    return o.astype(q.dtype)


def rel_bucket(rel):
    nb = NUM_BUCKETS // 2
    max_exact = nb // 2
    base = jnp.where(rel > 0, nb, 0)
    n = jnp.abs(rel)
    nf = jnp.maximum(n, 1).astype(jnp.float32)
    large = max_exact + (jnp.log(nf / max_exact) / math.log(MAX_DISTANCE / max_exact) * (nb - max_exact)).astype(jnp.int32)
    large = jnp.minimum(large, nb - 1)
    return base + jnp.where(n < max_exact, n, large)


def diff_branch(q, k, v, z, q_gain, k_gain, lq1, lk1, lq2, lk2, subln_gain, rel_bias, layer_idx):
    bsz, s, _ = q.shape
    lam_init = 0.8 - 0.6 * math.exp(-0.3 * layer_idx)
    lam = (jnp.exp(jnp.sum(lq1.astype(jnp.float32) * lk1.astype(jnp.float32)))
           - jnp.exp(jnp.sum(lq2.astype(jnp.float32) * lk2.astype(jnp.float32))) + lam_init)
    qn = rmsnorm(q.reshape(bsz, s, DIFF_HEADS, 2, DIFF_DK), q_gain) * (DIFF_DK ** -0.5)
    kn = rmsnorm(k.reshape(bsz, s, DIFF_HEADS, 2, DIFF_DK), k_gain)
    vh = v.reshape(bsz, s, DIFF_HEADS, DIFF_DV)
    nb = s // Q_BLOCK
    qb = qn.reshape(bsz, nb, Q_BLOCK, DIFF_HEADS, 2, DIFF_DK).transpose(1, 0, 3, 4, 2, 5)
    kt = kn.transpose(0, 2, 3, 1, 4)
    vt = vh.transpose(0, 2, 1, 3)
    k_pos = jnp.arange(s)

    def block(inp):
        q_i, blk = inp
        q_pos = blk * Q_BLOCK + jnp.arange(Q_BLOCK)
        bias = rel_bias[rel_bucket(k_pos[None, :] - q_pos[:, None])]
        bias = bias.astype(jnp.float32).transpose(2, 0, 1)
        logits = jnp.einsum('bhmqd,bhmkd->bhmqk', q_i, kt).astype(jnp.float32) + bias[None, :, None]
        visible = (k_pos[None, :] // CHUNK) <= (q_pos[:, None] // CHUNK)
        p = jax.nn.softmax(jnp.where(visible, logits, -jnp.inf), axis=-1)
        attn = p[:, :, 0] - lam * p[:, :, 1]
        return jnp.einsum('bhqk,bhkv->bhqv', attn.astype(vt.dtype), vt)

    o = lax.map(block, (qb, jnp.arange(nb)))
    o = o.transpose(1, 0, 3, 2, 4).reshape(bsz, s, DIFF_HEADS, DIFF_DV)
    o = rmsnorm(o, subln_gain).astype(jnp.float32) * (1.0 - lam_init)
    o = o.reshape(bsz, s, BRANCH_W) * jax.nn.silu(z.astype(jnp.float32))
    return o.astype(q.dtype)


def setup_inputs(seed: int = 0) -> dict:
    key = jax.random.key(seed)
    ks = jax.random.split(key, 20)
    f32 = jnp.float32
    x = jax.random.normal(ks[0], (BATCH, SEQ, D_MODEL), f32)
    norm_gain = 1.0 + 0.02 * jax.random.normal(ks[1], (DEPTH, D_MODEL), f32)
    w_in = jax.random.normal(ks[2], (DEPTH, D_MODEL, IN_COLS), f32) * (D_MODEL ** -0.5)
    ret_gn_gain = 1.0 + 0.02 * jax.random.normal(ks[3], (DEPTH, BRANCH_W), f32)
    gdn_conv_w = jax.random.normal(ks[4], (DEPTH, CONV_W, GDN_CONV_CH), f32) * (CONV_W ** -0.5)
    gdn_a_log = jnp.log(jax.random.uniform(ks[5], (DEPTH, GDN_HEADS), f32, 1.0, 16.0))
    dt = jnp.exp(jax.random.uniform(ks[6], (DEPTH, GDN_HEADS), f32, math.log(1e-3), math.log(1e-1)))
    gdn_dt_bias = dt + jnp.log(-jnp.expm1(-dt))
    gdn_norm_gain = 1.0 + 0.02 * jax.random.normal(ks[7], (DEPTH, GDN_DV), f32)
    diff_q_gain = 1.0 + 0.02 * jax.random.normal(ks[8], (DEPTH, DIFF_DK), f32)
    diff_k_gain = 1.0 + 0.02 * jax.random.normal(ks[9], (DEPTH, DIFF_DK), f32)
    diff_lambda_q1 = 0.1 * jax.random.normal(ks[10], (DEPTH, DIFF_DK), f32)
    diff_lambda_k1 = 0.1 * jax.random.normal(ks[11], (DEPTH, DIFF_DK), f32)
    diff_lambda_q2 = 0.1 * jax.random.normal(ks[12], (DEPTH, DIFF_DK), f32)
    diff_lambda_k2 = 0.1 * jax.random.normal(ks[13], (DEPTH, DIFF_DK), f32)
    diff_subln_gain = 1.0 + 0.02 * jax.random.normal(ks[14], (DEPTH, DIFF_DV), f32)
    rel_bias = 0.5 * jax.random.normal(ks[15], (NUM_BUCKETS, DIFF_HEADS), f32)
    w_branch = jax.random.normal(ks[16], (DEPTH, N_BRANCH, BRANCH_W, D_MODEL), f32) * (BRANCH_W ** -0.5)
    w_out = jax.random.normal(ks[17], (DEPTH, D_MODEL, D_MODEL), f32) * (D_MODEL ** -0.5)
    return {'x': x, 'norm_gain': norm_gain, 'w_in': w_in, 'ret_gn_gain': ret_gn_gain,
            'gdn_conv_w': gdn_conv_w, 'gdn_a_log': gdn_a_log, 'gdn_dt_bias': gdn_dt_bias,
            'gdn_norm_gain': gdn_norm_gain, 'diff_q_gain': diff_q_gain, 'diff_k_gain': diff_k_gain,
            'diff_lambda_q1': diff_lambda_q1, 'diff_lambda_k1': diff_lambda_k1,
            'diff_lambda_q2': diff_lambda_q2, 'diff_lambda_k2': diff_lambda_k2,
            'diff_subln_gain': diff_subln_gain, 'rel_bias': rel_bias,
            'w_branch': w_branch, 'w_out': w_out}


def reference(x, norm_gain, w_in, ret_gn_gain, gdn_conv_w, gdn_a_log, gdn_dt_bias, gdn_norm_gain,
              diff_q_gain, diff_k_gain, diff_lambda_q1, diff_lambda_k1, diff_lambda_q2, diff_lambda_k2,
              diff_subln_gain, rel_bias, w_branch, w_out):
    split_points = [int(c) for c in np.cumsum(IN_SIZES)[:-1]]
    bsz, s, _ = x.shape
    for l in range(DEPTH):
        h = rmsnorm(x, norm_gain[l])
        p = jnp.einsum('bsd,dc->bsc', h, w_in[l])
        (rq, rk, rv, rz, gq, gk, gv, gz, ga, gb, dq, dk, dv, dz, gate) = jnp.split(p, split_points, axis=-1)
        y_ret = retention_branch(rq, rk, rv, rz, ret_gn_gain[l])
        y_gdn = gdn_branch(gq, gk, gv, gz, ga, gb, gdn_conv_w[l], gdn_a_log[l], gdn_dt_bias[l], gdn_norm_gain[l])
        y_diff = diff_branch(dq, dk, dv, dz, diff_q_gain[l], diff_k_gain[l], diff_lambda_q1[l], diff_lambda_k1[l],
                             diff_lambda_q2[l], diff_lambda_k2[l], diff_subln_gain[l], rel_bias, l)
        y = jnp.stack([y_ret, y_gdn, y_diff], axis=2)
        branch = jnp.einsum('bsnw,nwd->bsnd', y, w_branch[l])
        gates = jax.nn.sigmoid(gate.reshape(bsz, s, N_BRANCH, D_MODEL))
        merged = jnp.sum(gates * branch, axis=2)
        x = x + jnp.einsum('bsd,de->bse', merged, w_out[l])
    return x
```

```python
import functools
import math

import numpy as np
import jax
import jax.numpy as jnp
from jax import lax
from jax.experimental import pallas as pl
from jax.experimental.pallas import tpu as pltpu

F32 = jnp.float32
BF16 = jnp.bfloat16

D_MODEL = 4096
EPS = 1e-6
BRANCH_W = 2048
N_BRANCH = 3

RET_HEADS = 8
RET_DK = 256
RET_DV = BRANCH_W // RET_HEADS
ROPE_BASE = 10000.0

GDN_HEADS = 16
GDN_DK = 128
GDN_DV = BRANCH_W // GDN_HEADS
CONV_W = 4

DIFF_HEADS = 8
DIFF_DK = 128
DIFF_DV = BRANCH_W // DIFF_HEADS
MASK_CHUNK = 64
NUM_BUCKETS = 32
MAX_DISTANCE = 128

AB_START = 16384
AB_COLS = 2 * GDN_HEADS
MAIN_COLS = 36864
COL_RET = 0
COL_GDN = 8192
COL_DIFF = 16384
COL_GATE = 24576

LANE = 128
SUBLANE = 8
SEQ_CHUNK = 256
NEG_MASK = -1e30
VMEM_LIMIT = 56 * 1024 * 1024


def _cparams(sem):
    return pltpu.CompilerParams(dimension_semantics=sem, vmem_limit_bytes=VMEM_LIMIT)


def _mm(a, b):
    return jnp.dot(a.astype(BF16), b.astype(BF16), preferred_element_type=F32)


def _mm_nt(a, b):
    return lax.dot_general(a.astype(BF16), b.astype(BF16), (((1,), (1,)), ((), ())),
                           preferred_element_type=F32)


def _silu(x):
    return x * jax.nn.sigmoid(x)


def _rmsnorm_kernel(x_ref, g_ref, o_ref):
    x = x_ref[...]
    ms = jnp.mean(x * x, axis=-1, keepdims=True)
    o_ref[...] = (x * lax.rsqrt(ms + EPS) * g_ref[...]).astype(o_ref.dtype)


def _rmsnorm(x2d, gain):
    t, d = x2d.shape
    tm = 256
    return pl.pallas_call(
        _rmsnorm_kernel,
        grid=(t // tm,),
        in_specs=[pl.BlockSpec((tm, d), lambda i: (i, 0)),
                  pl.BlockSpec((1, d), lambda i: (0, 0))],
        out_specs=pl.BlockSpec((tm, d), lambda i: (i, 0)),
        out_shape=jax.ShapeDtypeStruct((t, d), BF16),
        compiler_params=_cparams(("parallel",)),
        name="rmsnorm",
    )(x2d, gain.reshape(1, d))


def _matmul_kernel(a_ref, w_ref, o_ref):
    o_ref[...] = jnp.dot(a_ref[...], w_ref[...], preferred_element_type=F32).astype(o_ref.dtype)


def _matmul(a, w, out_dtype, tm, tn, name):
    m, k = a.shape
    n = w.shape[1]
    return pl.pallas_call(
        _matmul_kernel,
        grid=(n // tn, m // tm),
        in_specs=[pl.BlockSpec((tm, k), lambda j, i: (i, 0)),
                  pl.BlockSpec((k, tn), lambda j, i: (0, j))],
        out_specs=pl.BlockSpec((tm, tn), lambda j, i: (i, j)),
        out_shape=jax.ShapeDtypeStruct((m, n), out_dtype),
        compiler_params=_cparams(("parallel", "arbitrary")),
        name=name,
    )(a, w)


def _retention_kernel(lg_ref, q_ref, k_ref, v_ref, z_ref, cos_ref, sin_ref, gain_ref,
                      o_ref, state_ref):
    c = SEQ_CHUNK
    h = pl.program_id(1)

    @pl.when(pl.program_id(2) == 0)
    def _():
        state_ref[...] = jnp.zeros_like(state_ref)

    lg = lg_ref[h]
    cos = cos_ref[...]
    sin = sin_ref[...]
    half = RET_DK // 2

    def rot(x):
        x1 = x[:, :half]
        x2 = x[:, half:]
        return jnp.concatenate([x1 * cos - x2 * sin, x1 * sin + x2 * cos], axis=-1)

    q = rot(q_ref[0])
    k = rot(k_ref[0]) * (RET_DK ** -0.5)
    v = v_ref[0]

    ri = lax.broadcasted_iota(jnp.int32, (c, c), 0)
    ci = lax.broadcasted_iota(jnp.int32, (c, c), 1)
    rel = (ri - ci).astype(F32)
    dmat = jnp.where(rel >= 0, jnp.exp(jnp.maximum(rel, 0.0) * lg), 0.0)
    idx = lax.broadcasted_iota(jnp.int32, (c, 1), 0).astype(F32)
    q_decay = jnp.exp((idx + 1.0) * lg)
    k_decay = jnp.exp((c - 1.0 - idx) * lg)
    chunk_decay = jnp.exp(jnp.full((1, 1), float(c), F32) * lg)

    scores = _mm_nt(q, k) * dmat
    intra = _mm(scores, v)
    state = state_ref[...]
    inter = _mm(q, state) * q_decay
    state_ref[...] = state * chunk_decay + _mm((k * k_decay).T, v)

    o = intra + inter
    mu = jnp.mean(o, axis=-1, keepdims=True)
    var = jnp.mean(jnp.square(o - mu), axis=-1, keepdims=True)
    o = (o - mu) * lax.rsqrt(var + EPS) * gain_ref[...]
    o_ref[0] = (o * _silu(z_ref[0])).astype(o_ref.dtype)


def _retention(p_main, gn_gain, log_gamma, cos, sin):
    b, s, _ = p_main.shape
    c = SEQ_CHUNK
    qb = COL_RET // RET_DK
    kb = qb + RET_HEADS
    vb = kb + RET_HEADS
    zb = vb + RET_HEADS

    def col(base):
        return pl.BlockSpec((1, c, RET_DK), lambda bi, hi, ci: (bi, ci, base + hi))

    return pl.pallas_call(
        _retention_kernel,
        grid=(b, RET_HEADS, s // c),
        in_specs=[pl.BlockSpec(memory_space=pltpu.SMEM),
                  col(qb), col(kb), col(vb), col(zb),
                  pl.BlockSpec((c, RET_DK // 2), lambda bi, hi, ci: (ci, 0)),
                  pl.BlockSpec((c, RET_DK // 2), lambda bi, hi, ci: (ci, 0)),
                  pl.BlockSpec((1, RET_DV), lambda bi, hi, ci: (0, hi))],
        out_specs=pl.BlockSpec((1, c, RET_DV), lambda bi, hi, ci: (bi, ci, hi)),
        out_shape=jax.ShapeDtypeStruct((b, s, BRANCH_W), BF16),
        scratch_shapes=[pltpu.VMEM((RET_DK, RET_DV), F32)],
        compiler_params=_cparams(("parallel", "parallel", "arbitrary")),
        name="retention",
    )(log_gamma, p_main, p_main, p_main, p_main, cos, sin, gn_gain.reshape(1, BRANCH_W))


def _gdn_gate_kernel(ab_ref, alog_ref, dt_ref, gb_ref, gbt_ref):
    c = SEQ_CHUNK
    x = ab_ref[0]
    sp = jnp.maximum(x + dt_ref[...], 0.0) + jnp.log1p(jnp.exp(-jnp.abs(x + dt_ref[...])))
    g = -jnp.exp(alog_ref[...]) * sp
    ri = lax.broadcasted_iota(jnp.int32, (c, c), 0)
    ci = lax.broadcasted_iota(jnp.int32, (c, c), 1)
    tri = jnp.where(ri >= ci, 1.0, 0.0).astype(F32)
    gc = jnp.dot(tri, g, preferred_element_type=F32, precision=lax.Precision.HIGHEST)
    lane = lax.broadcasted_iota(jnp.int32, x.shape, 1)
    out = jnp.where(lane < GDN_HEADS, gc, jax.nn.sigmoid(x))
    gb_ref[0] = out
    gbt_ref[0] = out.T


def _gdn_gates(p_ab, a_log, dt_bias):
    b, s, _ = p_ab.shape
    c = SEQ_CHUNK
    pad = LANE - GDN_HEADS
    alog = jnp.pad(a_log, (0, pad)).reshape(1, LANE)
    dt = jnp.pad(dt_bias, (0, pad)).reshape(1, LANE)
    return pl.pallas_call(
        _gdn_gate_kernel,
        grid=(b, s // c),
        in_specs=[pl.BlockSpec((1, c, LANE), lambda bi, ci: (bi, ci, 0)),
                  pl.BlockSpec((1, LANE), lambda bi, ci: (0, 0)),
                  pl.BlockSpec((1, LANE), lambda bi, ci: (0, 0))],
        out_specs=[pl.BlockSpec((1, c, LANE), lambda bi, ci: (bi, ci, 0)),
                   pl.BlockSpec((1, LANE, c), lambda bi, ci: (bi, 0, ci))],
        out_shape=[jax.ShapeDtypeStruct((b, s, LANE), F32),
                   jax.ShapeDtypeStruct((b, LANE, s), F32)],
        compiler_params=_cparams(("parallel", "parallel")),
        name="gdn_gates",
    )(p_ab, alog, dt)


def _unit_lower_inverse(a, ri, ci):
    def same_block(shift):
        return (ri >> shift) == (ci >> shift)

    eye = jnp.where(ri == ci, 1.0, 0.0).astype(F32)
    a8 = jnp.where(same_block(3), a, 0.0)
    t = eye - a8
    a2 = _mm(a8, a8)
    t = t + _mm(t, a2)
    a4 = _mm(a2, a2)
    t = t + _mm(t, a4)
    shift = 3
    while (1 << shift) < a.shape[0]:
        off = jnp.where(same_block(shift + 1) & jnp.logical_not(same_block(shift)), a, 0.0)
        t = t - _mm(_mm(t, off), t)
        shift += 1
    return t


def _gdn_kernel(q_ref, k_ref, v_ref, z_ref, qh_ref, kh_ref, vh_ref, wq_ref, wk_ref, wv_ref,
                gb_ref, gbt_ref, gain_ref, o_ref, state_ref):
    c = SEQ_CHUNK
    h = pl.program_id(1)
    first = pl.program_id(2) == 0

    @pl.when(first)
    def _():
        state_ref[...] = jnp.zeros_like(state_ref)

    def conv_silu(x_ref, halo_ref, w_ref):
        halo = jnp.where(first, 0.0, halo_ref[0])
        xx = jnp.concatenate([halo, x_ref[0]], axis=0)
        w = w_ref[...]
        y = xx[SUBLANE:SUBLANE + c] * w[CONV_W - 1:CONV_W]
        for i in range(CONV_W - 1):
            off = SUBLANE - (CONV_W - 1) + i
            y = y + xx[off:off + c] * w[i:i + 1]
        return _silu(y)

    q = conv_silu(q_ref, qh_ref, wq_ref)
    k = conv_silu(k_ref, kh_ref, wk_ref)
    v = conv_silu(v_ref, vh_ref, wv_ref)
    q = q * lax.rsqrt(jnp.sum(q * q, axis=-1, keepdims=True) + EPS) * (GDN_DK ** -0.5)
    k = k * lax.rsqrt(jnp.sum(k * k, axis=-1, keepdims=True) + EPS)

    gb = gb_ref[0]
    lane = lax.broadcasted_iota(jnp.int32, gb.shape, 1)
    gc_col = jnp.sum(jnp.where(lane == h, gb, 0.0), axis=1, keepdims=True)
    beta = jnp.sum(jnp.where(lane == h + GDN_HEADS, gb, 0.0), axis=1, keepdims=True)
    gc_row = gbt_ref[0, pl.ds(h, 1), :]
    g_last = gc_col[c - 1:c, :]

    ri = lax.broadcasted_iota(jnp.int32, (c, c), 0)
    ci = lax.broadcasted_iota(jnp.int32, (c, c), 1)
    decay = jnp.where(ri >= ci, jnp.exp(jnp.minimum(gc_col - gc_row, 0.0)), 0.0)

    k_beta = k * beta
    kk = _mm_nt(k_beta, k)
    a_mat = jnp.where(ri > ci, kk * decay, 0.0)
    t_inv = _unit_lower_inverse(a_mat, ri, ci)
    e_gc = jnp.exp(gc_col)
    sol = _mm(t_inv, jnp.concatenate([v * beta, k_beta * e_gc], axis=-1))
    u = sol[:, :GDN_DV]
    w = sol[:, GDN_DV:]
    attn = _mm_nt(q, k) * decay
    q_g = q * e_gc
    k_g = k * jnp.exp(g_last - gc_col)

    state = state_ref[...]
    both = _mm(jnp.concatenate([w, q_g], axis=0), state)
    v_new = u - both[:c]
    o = both[c:] + _mm(attn, v_new)
    state_ref[...] = state * jnp.exp(g_last) + _mm(k_g.T, v_new)

    o = o * lax.rsqrt(jnp.mean(o * o, axis=-1, keepdims=True) + EPS) * gain_ref[...]
    o_ref[0] = (o * _silu(z_ref[0])).astype(o_ref.dtype)


def _gdn(p_main, conv_w, gb, gbt, norm_gain):
    b, s, _ = p_main.shape
    c = SEQ_CHUNK
    qb = COL_GDN // GDN_DK
    kb = qb + GDN_HEADS
    vb = kb + GDN_HEADS
    zb = vb + GDN_HEADS
    rows_per_chunk = c // SUBLANE

    def col(base):
        return pl.BlockSpec((1, c, GDN_DK), lambda bi, hi, ci: (bi, ci, base + hi))

    def halo(base):
        return pl.BlockSpec((1, SUBLANE, GDN_DK),
                            lambda bi, hi, ci: (bi, jnp.maximum(ci * rows_per_chunk - 1, 0), base + hi))

    def convw(base):
        return pl.BlockSpec((CONV_W, GDN_DK), lambda bi, hi, ci: (0, base + hi))

    return pl.pallas_call(
        _gdn_kernel,
        grid=(b, GDN_HEADS, s // c),
        in_specs=[col(qb), col(kb), col(vb), col(zb),
                  halo(qb), halo(kb), halo(vb),
                  convw(0), convw(GDN_HEADS), convw(2 * GDN_HEADS),
                  pl.BlockSpec((1, c, LANE), lambda bi, hi, ci: (bi, ci, 0)),
                  pl.BlockSpec((1, LANE, c), lambda bi, hi, ci: (bi, 0, ci)),
                  pl.BlockSpec((1, GDN_DV), lambda bi, hi, ci: (0, 0))],
        out_specs=pl.BlockSpec((1, c, GDN_DV), lambda bi, hi, ci: (bi, ci, hi)),
        out_shape=jax.ShapeDtypeStruct((b, s, BRANCH_W), BF16),
        scratch_shapes=[pltpu.VMEM((GDN_DK, GDN_DV), F32)],
        compiler_params=_cparams(("parallel", "parallel", "arbitrary")),
        name="gdn",
    )(p_main, p_main, p_main, p_main, p_main, p_main, p_main, conv_w, conv_w, conv_w,
      gb, gbt, norm_gain.reshape(1, GDN_DV))


def _qknorm_kernel(q_ref, k_ref, v_ref, qg_ref, kg_ref, qo_ref, ko_ref, vo_ref):
    def norm(x_ref, g_ref, o_ref, scale):
        g = g_ref[...]
        for i in range(BRANCH_W // DIFF_DK):
            x = x_ref[0, :, i * DIFF_DK:(i + 1) * DIFF_DK]
            y = x * lax.rsqrt(jnp.mean(x * x, axis=-1, keepdims=True) + EPS) * g
            o_ref[0, :, i * DIFF_DK:(i + 1) * DIFF_DK] = (y * scale).astype(o_ref.dtype)

    norm(q_ref, qg_ref, qo_ref, DIFF_DK ** -0.5)
    norm(k_ref, kg_ref, ko_ref, 1.0)
    vo_ref[...] = v_ref[...].astype(vo_ref.dtype)


def _qknorm(p_main, q_gain, k_gain):
    b, s, _ = p_main.shape
    c = SEQ_CHUNK
    base = COL_DIFF // BRANCH_W

    def col(j):
        return pl.BlockSpec((1, c, BRANCH_W), lambda bi, ci: (bi, ci, base + j))

    out = pl.BlockSpec((1, c, BRANCH_W), lambda bi, ci: (bi, ci, 0))
    gain = pl.BlockSpec((1, DIFF_DK), lambda bi, ci: (0, 0))
    shape = jax.ShapeDtypeStruct((b, s, BRANCH_W), BF16)
    return pl.pallas_call(
        _qknorm_kernel,
        grid=(b, s // c),
        in_specs=[col(0), col(1), col(2), gain, gain],
        out_specs=[out, out, out],
        out_shape=[shape, shape, shape],
        compiler_params=_cparams(("parallel", "parallel")),
        name="qknorm",
    )(p_main, p_main, p_main, q_gain.reshape(1, DIFF_DK), k_gain.reshape(1, DIFF_DK))


def _rel_bucket_np(rel):
    nb = NUM_BUCKETS // 2
    max_exact = nb // 2
    n = np.abs(rel)
    nf = np.maximum(n, 1).astype(np.float64)
    large = max_exact + (np.log(nf / max_exact) / math.log(MAX_DISTANCE / max_exact)
                         * (nb - max_exact)).astype(np.int32)
    large = np.minimum(large, nb - 1)
    return np.where(rel > 0, nb, 0) + np.where(n < max_exact, n, large)


FAR_BUCKET = NUM_BUCKETS // 2 - 1


def _bias_patterns():
    t = SEQ_CHUNK
    qpos = np.arange(t)[:, None]
    pats = []
    for koff in (0, -t):
        kpos = np.arange(t)[None, :] + koff
        bucket = _rel_bucket_np(kpos - qpos)
        visible = (kpos // MASK_CHUNK) <= (qpos // MASK_CHUNK)
        pats.append(np.where(visible, bucket, -1))
    return np.stack(pats).astype(np.int32)


def _bias_tile_kernel(rb_ref, pat_ref, o_ref):
    h = pl.program_id(0)
    pat = pat_ref[0]
    far = rb_ref[FAR_BUCKET, h]
    acc = jnp.full(pat.shape, NEG_MASK, F32)
    for bkt in range(NUM_BUCKETS):
        acc = jnp.where(pat == bkt, rb_ref[bkt, h] - far, acc)
    o_ref[0, 0] = acc


def _bias_tiles(rel_bias):
    t = SEQ_CHUNK
    pats = np.concatenate([_bias_patterns(), np.full((1, t, t), FAR_BUCKET, np.int32)])
    return pl.pallas_call(
        _bias_tile_kernel,
        grid=(DIFF_HEADS, pats.shape[0]),
        in_specs=[pl.BlockSpec(memory_space=pltpu.SMEM),
                  pl.BlockSpec((1, t, t), lambda hi, pi: (pi, 0, 0))],
        out_specs=pl.BlockSpec((1, 1, t, t), lambda hi, pi: (hi, pi, 0, 0)),
        out_shape=jax.ShapeDtypeStruct((DIFF_HEADS, pats.shape[0], t, t), F32),
        compiler_params=_cparams(("parallel", "parallel")),
        name="bias_tiles",
    )(rel_bias, jnp.asarray(pats))


def _diff_attn_kernel(qi_ref, ki_ref, q_ref, k_ref, v_ref, bias_ref, z_ref,
                      lq1_ref, lk1_ref, lq2_ref, lk2_ref, gain_ref, o_ref,
                      m1_ref, l1_ref, a1_ref, m2_ref, l2_ref, a2_ref, *, lam_init):
    step = pl.program_id(2)
    qi = qi_ref[step]
    ki = ki_ref[step]

    @pl.when(ki == 0)
    def _():
        for m_ref, l_ref, a_ref in ((m1_ref, l1_ref, a1_ref), (m2_ref, l2_ref, a2_ref)):
            m_ref[...] = jnp.full_like(m_ref, NEG_MASK)
            l_ref[...] = jnp.zeros_like(l_ref)
            a_ref[...] = jnp.zeros_like(a_ref)

    bias = bias_ref[0, 0]
    v = v_ref[0]
    for idx, (m_ref, l_ref, a_ref) in enumerate(((m1_ref, l1_ref, a1_ref), (m2_ref, l2_ref, a2_ref))):
        qm = q_ref[0, :, idx * DIFF_DK:(idx + 1) * DIFF_DK]
        km = k_ref[0, :, idx * DIFF_DK:(idx + 1) * DIFF_DK]
        s = _mm_nt(qm, km) + bias
        m_old = m_ref[...]
        m_new = jnp.maximum(m_old, jnp.max(s, axis=-1, keepdims=True))
        alpha = jnp.exp(m_old - m_new)
        p = jnp.exp(s - m_new)
        l_ref[...] = alpha * l_ref[...] + jnp.sum(p, axis=-1, keepdims=True)
        a_ref[...] = alpha * a_ref[...] + _mm(p, v)
        m_ref[...] = m_new

    @pl.when(ki == qi)
    def _():
        lam = (jnp.exp(jnp.sum(lq1_ref[...] * lk1_ref[...], axis=-1, keepdims=True))
               - jnp.exp(jnp.sum(lq2_ref[...] * lk2_ref[...], axis=-1, keepdims=True)) + lam_init)
        o = a1_ref[...] / l1_ref[...] - lam * (a2_ref[...] / l2_ref[...])
        o = o * lax.rsqrt(jnp.mean(o * o, axis=-1, keepdims=True) + EPS) * gain_ref[...]
        o = o * (1.0 - lam_init)
        o_ref[0] = (o * _silu(z_ref[0])).astype(o_ref.dtype)


def _diff_attention(qn, kn, vb, p_main, bias_tiles, lq1, lk1, lq2, lk2, subln_gain, lam_init):
    b, s, _ = qn.shape
    t = SEQ_CHUNK
    nq = s // t
    pairs = [(q, k) for q in range(nq) for k in range(q + 1)]
    qi_tab = jnp.asarray(np.array([p[0] for p in pairs], np.int32))
    ki_tab = jnp.asarray(np.array([p[1] for p in pairs], np.int32))
    hw = 2 * DIFF_DK
    zb = (COL_DIFF + 3 * BRANCH_W) // DIFF_DV

    def bias_map(bi, hi, st, qi, ki):
        far = qi[st] - ki[st]
        return (hi, jnp.minimum(far, 2), 0, 0)

    vec = pl.BlockSpec((1, DIFF_DK), lambda bi, hi, st, qi, ki: (0, 0))
    grid_spec = pltpu.PrefetchScalarGridSpec(
        num_scalar_prefetch=2,
        grid=(b, DIFF_HEADS, len(pairs)),
        in_specs=[pl.BlockSpec((1, t, hw), lambda bi, hi, st, qi, ki: (bi, qi[st], hi)),
                  pl.BlockSpec((1, t, hw), lambda bi, hi, st, qi, ki: (bi, ki[st], hi)),
                  pl.BlockSpec((1, t, DIFF_DV), lambda bi, hi, st, qi, ki: (bi, ki[st], hi)),
                  pl.BlockSpec((1, 1, t, t), bias_map),
                  pl.BlockSpec((1, t, DIFF_DV), lambda bi, hi, st, qi, ki: (bi, qi[st], zb + hi)),
                  vec, vec, vec, vec,
                  pl.BlockSpec((1, DIFF_DV), lambda bi, hi, st, qi, ki: (0, 0))],
        out_specs=pl.BlockSpec((1, t, DIFF_DV), lambda bi, hi, st, qi, ki: (bi, qi[st], hi)),
        scratch_shapes=[pltpu.VMEM((t, 1), F32), pltpu.VMEM((t, 1), F32), pltpu.VMEM((t, DIFF_DV), F32),
                        pltpu.VMEM((t, 1), F32), pltpu.VMEM((t, 1), F32), pltpu.VMEM((t, DIFF_DV), F32)],
    )
    r = lambda x: x.reshape(1, -1)
    return pl.pallas_call(
        functools.partial(_diff_attn_kernel, lam_init=lam_init),
        grid_spec=grid_spec,
        out_shape=jax.ShapeDtypeStruct((b, s, BRANCH_W), BF16),
        compiler_params=_cparams(("parallel", "parallel", "arbitrary")),
        name="diff_attention",
    )(qi_tab, ki_tab, qn, kn, vb, bias_tiles, p_main, r(lq1), r(lk1), r(lq2), r(lk2), r(subln_gain))


def _merge_kernel(y0_ref, y1_ref, y2_ref, w0_ref, w1_ref, w2_ref, g0_ref, g1_ref, g2_ref, o_ref):
    acc = None
    for y_ref, w_ref, g_ref in ((y0_ref, w0_ref, g0_ref), (y1_ref, w1_ref, g1_ref), (y2_ref, w2_ref, g2_ref)):
        term = jax.nn.sigmoid(g_ref[...]) * jnp.dot(y_ref[...], w_ref[0], preferred_element_type=F32)
        acc = term if acc is None else acc + term
    o_ref[...] = acc.astype(o_ref.dtype)


def _merge(ys, w_branch, p_main2d):
    t = ys[0].shape[0]
    tm, tn = 512, 512
    gate_base = COL_GATE // tn

    def wspec(n):
        return pl.BlockSpec((1, BRANCH_W, tn), lambda j, i: (n, 0, j))

    def gspec(n):
        return pl.BlockSpec((tm, tn), lambda j, i: (i, gate_base + n * (D_MODEL // tn) + j))

    yspec = pl.BlockSpec((tm, BRANCH_W), lambda j, i: (i, 0))
    return pl.pallas_call(
        _merge_kernel,
        grid=(D_MODEL // tn, t // tm),
        in_specs=[yspec, yspec, yspec, wspec(0), wspec(1), wspec(2), gspec(0), gspec(1), gspec(2)],
        out_specs=pl.BlockSpec((tm, tn), lambda j, i: (i, j)),
        out_shape=jax.ShapeDtypeStruct((t, D_MODEL), BF16),
        compiler_params=_cparams(("parallel", "arbitrary")),
        name="merge",
    )(*ys, w_branch, w_branch, w_branch, p_main2d, p_main2d, p_main2d)


def _out_proj_kernel(a_ref, w_ref, x_ref, o_ref):
    o_ref[...] = x_ref[...] + jnp.dot(a_ref[...], w_ref[...], preferred_element_type=F32)


def _out_proj(merged, w_out, x2d):
    t = merged.shape[0]
    tm, tn = 512, 1024
    return pl.pallas_call(
        _out_proj_kernel,
        grid=(D_MODEL // tn, t // tm),
        in_specs=[pl.BlockSpec((tm, D_MODEL), lambda j, i: (i, 0)),
                  pl.BlockSpec((D_MODEL, tn), lambda j, i: (0, j)),
                  pl.BlockSpec((tm, tn), lambda j, i: (i, j))],
        out_specs=pl.BlockSpec((tm, tn), lambda j, i: (i, j)),
        out_shape=jax.ShapeDtypeStruct((t, D_MODEL), F32),
        compiler_params=_cparams(("parallel", "arbitrary")),
        name="out_proj",
    )(merged, w_out, x2d)


def _rotary_tables(s):
    half = RET_DK // 2
    inv = ROPE_BASE ** (-np.arange(half, dtype=np.float64) / half)
    ang = np.arange(s, dtype=np.float64)[:, None] * inv[None, :]
    return jnp.asarray(np.cos(ang), F32), jnp.asarray(np.sin(ang), F32)


def _layer(x, layer_idx, norm_gain, w_in, ret_gn_gain, gdn_conv_w, gdn_a_log, gdn_dt_bias,
           gdn_norm_gain, diff_q_gain, diff_k_gain, lq1, lk1, lq2, lk2, diff_subln_gain,
           bias_tiles, w_branch, w_out, log_gamma, cos, sin):
    b, s, d = x.shape
    t = b * s
    x2d = x.reshape(t, d)
    w_main = jnp.concatenate([w_in[:, :AB_START], w_in[:, AB_START + AB_COLS:]], axis=1).astype(BF16)
    w_ab = jnp.pad(w_in[:, AB_START:AB_START + AB_COLS], ((0, 0), (0, LANE - AB_COLS))).astype(BF16)

    h = _rmsnorm(x2d, norm_gain)
    p_main2d = _matmul(h, w_main, F32, 512, 1024, "in_proj")
    p_ab = _matmul(h, w_ab, F32, 512, LANE, "in_proj_ab")
    p_main = p_main2d.reshape(b, s, MAIN_COLS)

    y_ret = _retention(p_main, ret_gn_gain, log_gamma, cos, sin)
    gb, gbt = _gdn_gates(p_ab.reshape(b, s, LANE), gdn_a_log, gdn_dt_bias)
    y_gdn = _gdn(p_main, gdn_conv_w, gb, gbt, gdn_norm_gain)
    qn, kn, vb = _qknorm(p_main, diff_q_gain, diff_k_gain)
    lam_init = 0.8 - 0.6 * math.exp(-0.3 * layer_idx)
    y_diff = _diff_attention(qn, kn, vb, p_main, bias_tiles, lq1, lk1, lq2, lk2,
                             diff_subln_gain, lam_init)

    ys = [y.reshape(t, BRANCH_W) for y in (y_ret, y_gdn, y_diff)]
    merged = _merge(ys, w_branch.astype(BF16), p_main2d)
    return _out_proj(merged, w_out.astype(BF16), x2d).reshape(b, s, d)


def kernel(x, norm_gain, w_in, ret_gn_gain, gdn_conv_w, gdn_a_log, gdn_dt_bias, gdn_norm_gain,
           diff_q_gain, diff_k_gain, diff_lambda_q1, diff_lambda_k1, diff_lambda_q2, diff_lambda_k2,
           diff_subln_gain, rel_bias, w_branch, w_out):
    depth = w_in.shape[0]
    s = x.shape[1]
    log_gamma = jnp.asarray(np.log(1.0 - 2.0 ** (-5.0 - np.arange(RET_HEADS, dtype=np.float64))), F32)
    cos, sin = _rotary_tables(s)
    bias_tiles = _bias_tiles(rel_bias)
    for l in range(depth):
        x = _layer(x, l, norm_gain[l], w_in[l], ret_gn_gain[l], gdn_conv_w[l], gdn_a_log[l],
                   gdn_dt_bias[l], gdn_norm_gain[l], diff_q_gain[l], diff_k_gain[l],
                   diff_lambda_q1[l], diff_lambda_k1[l], diff_lambda_q2[l], diff_lambda_k2[l],
                   diff_subln_gain[l], bias_tiles, w_branch[l], w_out[l], log_gamma, cos, sin)
    return x
```

```python
import functools
import math

import numpy as np
import jax
import jax.numpy as jnp
from jax import lax
from jax.experimental import pallas as pl
from jax.experimental.pallas import tpu as pltpu

F32 = jnp.float32
BF16 = jnp.bfloat16

D_MODEL = 4096
EPS = 1e-6
BRANCH_W = 2048
N_BRANCH = 3

RET_HEADS = 8
RET_DK = 256
RET_DV = BRANCH_W // RET_HEADS
ROPE_BASE = 10000.0

GDN_HEADS = 16
GDN_DK = 128
GDN_DV = BRANCH_W // GDN_HEADS
CONV_W = 4

DIFF_HEADS = 8
DIFF_DK = 128
DIFF_DV = BRANCH_W // DIFF_HEADS
MASK_CHUNK = 64
NUM_BUCKETS = 32
MAX_DISTANCE = 128

AB_START = 16384
AB_COLS = 2 * GDN_HEADS
MAIN_COLS = 36864
COL_RET = 0
COL_GDN = 8192
COL_DIFF = 16384
COL_GATE = 24576

LANE = 128
SUBLANE = 8
SEQ_CHUNK = 256
NEG_MASK = -1e30
LOG2E = math.log2(math.e)
GDN_HB = 4
ATT_TQ = 512
ATT_TK = 256
ATT_UNROLL = 2
ATT_NEAR = ATT_TQ // ATT_TK + 1
VMEM_LIMIT = 56 * 1024 * 1024


def _cparams(sem):
    return pltpu.CompilerParams(dimension_semantics=sem, vmem_limit_bytes=VMEM_LIMIT)


def _dot(a, b):
    return jnp.dot(a, b, preferred_element_type=F32)


def _mm(a, b):
    return _dot(a.astype(BF16), b.astype(BF16))


def _mm_nt(a, b):
    return lax.dot_general(a.astype(BF16), b.astype(BF16), (((1,), (1,)), ((), ())),
                           preferred_element_type=F32)


def _silu(x):
    return x * jax.nn.sigmoid(x)


def _rmsnorm_kernel(x_ref, g_ref, o_ref):
    x = x_ref[...]
    ms = jnp.mean(x * x, axis=-1, keepdims=True)
    o_ref[...] = (x * lax.rsqrt(ms + EPS) * g_ref[...]).astype(o_ref.dtype)


def _rmsnorm(x2d, gain):
    t, d = x2d.shape
    tm = 256
    return pl.pallas_call(
        _rmsnorm_kernel,
        grid=(t // tm,),
        in_specs=[pl.BlockSpec((tm, d), lambda i: (i, 0)),
                  pl.BlockSpec((1, d), lambda i: (0, 0))],
        out_specs=pl.BlockSpec((tm, d), lambda i: (i, 0)),
        out_shape=jax.ShapeDtypeStruct((t, d), BF16),
        compiler_params=_cparams(("parallel",)),
        name="rmsnorm",
    )(x2d, gain.reshape(1, d))


def _matmul_kernel(a_ref, w_ref, o_ref):
    o_ref[...] = _dot(a_ref[...], w_ref[...].astype(BF16)).astype(o_ref.dtype)


def _matmul(a, w, layer, col_block, n, out_dtype, tm, tn, name):
    m, k = a.shape
    return pl.pallas_call(
        _matmul_kernel,
        grid=(n // tn, m // tm),
        in_specs=[pl.BlockSpec((tm, k), lambda j, i: (i, 0)),
                  pl.BlockSpec((None, k, tn), lambda j, i: (layer, 0, col_block + j))],
        out_specs=pl.BlockSpec((tm, tn), lambda j, i: (i, j)),
        out_shape=jax.ShapeDtypeStruct((m, n), out_dtype),
        compiler_params=_cparams(("parallel", "arbitrary")),
        name=name,
    )(a, w)


def _retention_kernel(lg_ref, q_ref, k_ref, v_ref, z_ref, cos_ref, sin_ref, gain_ref,
                      o_ref, state_ref):
    c = SEQ_CHUNK
    h = pl.program_id(1)

    @pl.when(pl.program_id(2) == 0)
    def _():
        state_ref[...] = jnp.zeros_like(state_ref)

    lg = lg_ref[h]
    cos = cos_ref[...]
    sin = sin_ref[...]
    half = RET_DK // 2

    def rot(x):
        x1 = x[:, :half]
        x2 = x[:, half:]
        return jnp.concatenate([x1 * cos - x2 * sin, x1 * sin + x2 * cos], axis=-1)

    q = rot(q_ref[0])
    k = rot(k_ref[0]) * (RET_DK ** -0.5)
    v = v_ref[0]

    ri = lax.broadcasted_iota(jnp.int32, (c, c), 0)
    ci = lax.broadcasted_iota(jnp.int32, (c, c), 1)
    rel = (ri - ci).astype(F32)
    dmat = jnp.where(rel >= 0, jnp.exp(jnp.maximum(rel, 0.0) * lg), 0.0)
    idx = lax.broadcasted_iota(jnp.int32, (c, 1), 0).astype(F32)
    q_decay = jnp.exp((idx + 1.0) * lg)
    k_decay = jnp.exp((c - 1.0 - idx) * lg)
    chunk_decay = jnp.exp(jnp.full((1, 1), float(c), F32) * lg)

    scores = _mm_nt(q, k) * dmat
    intra = _mm(scores, v)
    state = state_ref[...]
    inter = _mm(q, state) * q_decay
    state_ref[...] = state * chunk_decay + _mm((k * k_decay).T, v)

    o = intra + inter
    mu = jnp.mean(o, axis=-1, keepdims=True)
    var = jnp.mean(jnp.square(o - mu), axis=-1, keepdims=True)
    o = (o - mu) * lax.rsqrt(var + EPS) * gain_ref[...]
    o_ref[0] = (o * _silu(z_ref[0])).astype(o_ref.dtype)


def _retention(p_main, gn_gain, log_gamma, cos, sin):
    b, s, _ = p_main.shape
    c = SEQ_CHUNK
    qb = COL_RET // RET_DK
    kb = qb + RET_HEADS
    vb = kb + RET_HEADS
    zb = vb + RET_HEADS

    def col(base):
        return pl.BlockSpec((1, c, RET_DK), lambda bi, hi, ci: (bi, ci, base + hi))

    return pl.pallas_call(
        _retention_kernel,
        grid=(b, RET_HEADS, s // c),
        in_specs=[pl.BlockSpec(memory_space=pltpu.SMEM),
                  col(qb), col(kb), col(vb), col(zb),
                  pl.BlockSpec((c, RET_DK // 2), lambda bi, hi, ci: (ci, 0)),
                  pl.BlockSpec((c, RET_DK // 2), lambda bi, hi, ci: (ci, 0)),
                  pl.BlockSpec((1, RET_DV), lambda bi, hi, ci: (0, hi))],
        out_specs=pl.BlockSpec((1, c, RET_DV), lambda bi, hi, ci: (bi, ci, hi)),
        out_shape=jax.ShapeDtypeStruct((b, s, BRANCH_W), BF16),
        scratch_shapes=[pltpu.VMEM((RET_DK, RET_DV), F32)],
        compiler_params=_cparams(("parallel", "parallel", "arbitrary")),
        name="retention",
    )(log_gamma, p_main, p_main, p_main, p_main, cos, sin, gn_gain.reshape(1, BRANCH_W))


def _gdn_gate_kernel(ab_ref, alog_ref, dt_ref, gb_ref, gbt_ref):
    c = SEQ_CHUNK
    x = ab_ref[0]
    sp = jnp.maximum(x + dt_ref[...], 0.0) + jnp.log1p(jnp.exp(-jnp.abs(x + dt_ref[...])))
    g = -jnp.exp(alog_ref[...]) * sp
    ri = lax.broadcasted_iota(jnp.int32, (c, c), 0)
    ci = lax.broadcasted_iota(jnp.int32, (c, c), 1)
    tri = jnp.where(ri >= ci, 1.0, 0.0).astype(F32)
    gc = jnp.dot(tri, g, preferred_element_type=F32, precision=lax.Precision.HIGHEST)
    lane = lax.broadcasted_iota(jnp.int32, x.shape, 1)
    out = jnp.where(lane < GDN_HEADS, gc, jax.nn.sigmoid(x))
    gb_ref[0] = out
    gbt_ref[0] = out.T


def _gdn_gates(p_ab, a_log, dt_bias):
    b, s, _ = p_ab.shape
    c = SEQ_CHUNK
    pad = LANE - GDN_HEADS
    alog = jnp.pad(a_log, (0, pad)).reshape(1, LANE)
    dt = jnp.pad(dt_bias, (0, pad)).reshape(1, LANE)
    return pl.pallas_call(
        _gdn_gate_kernel,
        grid=(b, s // c),
        in_specs=[pl.BlockSpec((1, c, LANE), lambda bi, ci: (bi, ci, 0)),
                  pl.BlockSpec((1, LANE), lambda bi, ci: (0, 0)),
                  pl.BlockSpec((1, LANE), lambda bi, ci: (0, 0))],
        out_specs=[pl.BlockSpec((1, c, LANE), lambda bi, ci: (bi, ci, 0)),
                   pl.BlockSpec((1, LANE, c), lambda bi, ci: (bi, 0, ci))],
        out_shape=[jax.ShapeDtypeStruct((b, s, LANE), F32),
                   jax.ShapeDtypeStruct((b, LANE, s), F32)],
        compiler_params=_cparams(("parallel", "parallel")),
        name="gdn_gates",
    )(p_ab, alog, dt)


def _block_levels(c):
    idx = np.arange(c)
    x = idx[:, None] ^ idx[None, :]
    return np.where(x > 0, np.floor(np.log2(np.maximum(x, 1))) + 1, 0).astype(np.float32)


def _unit_lower_inverse(a, lvl, eye):
    n = range(len(a))
    ab = [x.astype(BF16) for x in a]
    zero = jnp.zeros_like(ab[0])
    base_shift = 3
    a8 = [jnp.where(lvl <= base_shift, ab[i], zero) for i in n]
    t = [eye - a8[i].astype(F32) for i in n]
    a2 = [_dot(a8[i], a8[i]) for i in n]
    t = [t[i] + _mm(t[i], a2[i]) for i in n]
    a4 = [_mm(a2[i], a2[i]) for i in n]
    tb = [(t[i] + _mm(t[i], a4[i])).astype(BF16) for i in n]
    shift = base_shift
    while (1 << shift) < a[0].shape[0]:
        off = [jnp.where(lvl == shift + 1, ab[i], zero) for i in n]
        x = [_dot(tb[i], off[i]).astype(BF16) for i in n]
        tb = [tb[i] - _dot(x[i], tb[i]).astype(BF16) for i in n]
        shift += 1
    return tb


def _gdn_kernel(q_ref, k_ref, v_ref, z_ref, qh_ref, kh_ref, vh_ref, wq_ref, wk_ref, wv_ref,
                gb_ref, gbt_ref, lvl_ref, gain_ref, o_ref, state_ref):
    c = SEQ_CHUNK
    first = pl.program_id(2) == 0

    @pl.when(first)
    def _():
        state_ref[...] = jnp.zeros_like(state_ref)

    gb = gb_ref[0]
    lane = lax.broadcasted_iota(jnp.int32, gb.shape, 1)
    ri = lax.broadcasted_iota(jnp.int32, (c, c), 0)
    ci = lax.broadcasted_iota(jnp.int32, (c, c), 1)
    eye = jnp.where(ri == ci, 1.0, 0.0).astype(F32)
    lvl = lvl_ref[...]

    def conv_silu(x_ref, halo_ref, w_ref, sl):
        halo = jnp.where(first, 0.0, halo_ref[0, :, sl])
        xx = jnp.concatenate([halo, x_ref[0, :, sl]], axis=0)
        w = w_ref[:, sl]
        y = xx[SUBLANE:SUBLANE + c] * w[CONV_W - 1:CONV_W]
        for i in range(CONV_W - 1):
            off = SUBLANE - (CONV_W - 1) + i
            y = y + xx[off:off + c] * w[i:i + 1]
        return _silu(y)

    heads = range(GDN_HB)
    sls = [slice(j * GDN_DK, (j + 1) * GDN_DK) for j in heads]
    hs = [pl.program_id(1) * GDN_HB + j for j in heads]
    q = [conv_silu(q_ref, qh_ref, wq_ref, sl) for sl in sls]
    k = [conv_silu(k_ref, kh_ref, wk_ref, sl) for sl in sls]
    v = [conv_silu(v_ref, vh_ref, wv_ref, sl) for sl in sls]
    q = [x * lax.rsqrt(jnp.sum(x * x, axis=-1, keepdims=True) + EPS) * (GDN_DK ** -0.5) for x in q]
    k = [x * lax.rsqrt(jnp.sum(x * x, axis=-1, keepdims=True) + EPS) for x in k]

    gc_col = [jnp.sum(jnp.where(lane == h, gb, 0.0), axis=1, keepdims=True) for h in hs]
    beta = [jnp.sum(jnp.where(lane == h + GDN_HEADS, gb, 0.0), axis=1, keepdims=True) for h in hs]
    gc_row = [gbt_ref[0, pl.ds(h, 1), :] for h in hs]
    g_last = [x[c - 1:c, :] for x in gc_col]
    decay = [jnp.where(ri >= ci, jnp.exp(jnp.minimum(gc_col[j] - gc_row[j], 0.0)), 0.0) for j in heads]

    kb = [x.astype(BF16) for x in k]
    k_beta = [k[j] * beta[j] for j in heads]
    kk = [_mm_nt(k_beta[j], kb[j]) for j in heads]
    a_mat = [jnp.where(ri > ci, kk[j] * decay[j], 0.0) for j in heads]
    t_inv = _unit_lower_inverse(a_mat, lvl, eye)
    e_gc = [jnp.exp(x) for x in gc_col]
    rhs = [jnp.concatenate([v[j] * beta[j], k_beta[j] * e_gc[j]], axis=-1).astype(BF16) for j in heads]
    sol = [_dot(t_inv[j], rhs[j]) for j in heads]
    attn = [_mm_nt(q[j], kb[j]) * decay[j] for j in heads]
    q_g = [q[j] * e_gc[j] for j in heads]
    k_g = [k[j] * jnp.exp(g_last[j] - gc_col[j]) for j in heads]

    state = [state_ref[j] for j in heads]
    both = [_mm(jnp.concatenate([sol[j][:, GDN_DV:], q_g[j]], axis=0), state[j]) for j in heads]
    v_new = [(sol[j][:, :GDN_DV] - both[j][:c]).astype(BF16) for j in heads]
    o = [both[j][c:] + _mm(attn[j], v_new[j]) for j in heads]
    for j in heads:
        state_ref[j] = state[j] * jnp.exp(g_last[j]) + _mm(k_g[j].T, v_new[j])
    for j in heads:
        y = o[j] * lax.rsqrt(jnp.mean(o[j] * o[j], axis=-1, keepdims=True) + EPS) * gain_ref[...]
        o_ref[0, :, sls[j]] = (y * _silu(z_ref[0, :, sls[j]])).astype(o_ref.dtype)


def _gdn(p_main, conv_w, gb, gbt, norm_gain):
    b, s, _ = p_main.shape
    c = SEQ_CHUNK
    w = GDN_HB * GDN_DK
    groups = GDN_HEADS // GDN_HB
    qb = COL_GDN // w
    kb = qb + groups
    vb = kb + groups
    zb = vb + groups
    rows_per_chunk = c // SUBLANE

    def col(base):
        return pl.BlockSpec((1, c, w), lambda bi, hi, ci: (bi, ci, base + hi))

    def halo(base):
        return pl.BlockSpec((1, SUBLANE, w),
                            lambda bi, hi, ci: (bi, jnp.maximum(ci * rows_per_chunk - 1, 0), base + hi))

    def convw(base):
        return pl.BlockSpec((CONV_W, w), lambda bi, hi, ci: (0, base + hi))

    lvl = jnp.asarray(_block_levels(c), BF16)
    return pl.pallas_call(
        _gdn_kernel,
        grid=(b, groups, s // c),
        in_specs=[col(qb), col(kb), col(vb), col(zb),
                  halo(qb), halo(kb), halo(vb),
                  convw(0), convw(groups), convw(2 * groups),
                  pl.BlockSpec((1, c, LANE), lambda bi, hi, ci: (bi, ci, 0)),
                  pl.BlockSpec((1, LANE, c), lambda bi, hi, ci: (bi, 0, ci)),
                  pl.BlockSpec((c, c), lambda bi, hi, ci: (0, 0)),
                  pl.BlockSpec((1, GDN_DV), lambda bi, hi, ci: (0, 0))],
        out_specs=pl.BlockSpec((1, c, w), lambda bi, hi, ci: (bi, ci, hi)),
        out_shape=jax.ShapeDtypeStruct((b, s, BRANCH_W), BF16),
        scratch_shapes=[pltpu.VMEM((GDN_HB, GDN_DK, GDN_DV), F32)],
        compiler_params=_cparams(("parallel", "parallel", "arbitrary")),
        name="gdn",
    )(p_main, p_main, p_main, p_main, p_main, p_main, p_main, conv_w, conv_w, conv_w,
      gb, gbt, lvl, norm_gain.reshape(1, GDN_DV))


def _qknorm_kernel(q_ref, k_ref, v_ref, qg_ref, kg_ref, qo_ref, ko_ref, vo_ref):
    def norm(x_ref, g_ref, o_ref, scale):
        g = g_ref[...]
        for i in range(BRANCH_W // DIFF_DK):
            x = x_ref[0, :, i * DIFF_DK:(i + 1) * DIFF_DK]
            y = x * lax.rsqrt(jnp.mean(x * x, axis=-1, keepdims=True) + EPS) * g
            o_ref[0, :, i * DIFF_DK:(i + 1) * DIFF_DK] = (y * scale).astype(o_ref.dtype)

    norm(q_ref, qg_ref, qo_ref, DIFF_DK ** -0.5 * LOG2E)
    norm(k_ref, kg_ref, ko_ref, 1.0)
    vo_ref[...] = v_ref[...].astype(vo_ref.dtype)


def _qknorm(p_main, q_gain, k_gain):
    b, s, _ = p_main.shape
    c = SEQ_CHUNK
    base = COL_DIFF // BRANCH_W

    def col(j):
        return pl.BlockSpec((1, c, BRANCH_W), lambda bi, ci: (bi, ci, base + j))

    out = pl.BlockSpec((1, c, BRANCH_W), lambda bi, ci: (bi, ci, 0))
    gain = pl.BlockSpec((1, DIFF_DK), lambda bi, ci: (0, 0))
    shape = jax.ShapeDtypeStruct((b, s, BRANCH_W), BF16)
    return pl.pallas_call(
        _qknorm_kernel,
        grid=(b, s // c),
        in_specs=[col(0), col(1), col(2), gain, gain],
        out_specs=[out, out, out],
        out_shape=[shape, shape, shape],
        compiler_params=_cparams(("parallel", "parallel")),
        name="qknorm",
    )(p_main, p_main, p_main, q_gain.reshape(1, DIFF_DK), k_gain.reshape(1, DIFF_DK))


def _rel_bucket_np(rel):
    nb = NUM_BUCKETS // 2
    max_exact = nb // 2
    n = np.abs(rel)
    nf = np.maximum(n, 1).astype(np.float64)
    large = max_exact + (np.log(nf / max_exact) / math.log(MAX_DISTANCE / max_exact)
                         * (nb - max_exact)).astype(np.int32)
    large = np.minimum(large, nb - 1)
    return np.where(rel > 0, nb, 0) + np.where(n < max_exact, n, large)


FAR_BUCKET = NUM_BUCKETS // 2 - 1


def _bias_patterns():
    qpos = np.arange(ATT_TQ)[:, None]
    pats = []
    for j in range(ATT_NEAR):
        kpos = np.arange(ATT_TK)[None, :] + (j - 1) * ATT_TK
        bucket = _rel_bucket_np(kpos - qpos)
        visible = (kpos // MASK_CHUNK) <= (qpos // MASK_CHUNK)
        pats.append(np.where(visible, bucket, -1))
    return np.stack(pats).astype(np.int32)


def _bias_tile_kernel(rb_ref, pat_ref, o_ref):
    h = pl.program_id(0)
    pat = pat_ref[0]
    far = rb_ref[FAR_BUCKET, h]
    acc = jnp.full(pat.shape, NEG_MASK, F32)
    for bkt in range(NUM_BUCKETS):
        acc = jnp.where(pat == bkt, (rb_ref[bkt, h] - far) * LOG2E, acc)
    o_ref[0, 0] = acc


def _bias_tiles(rel_bias):
    pats = _bias_patterns()
    return pl.pallas_call(
        _bias_tile_kernel,
        grid=(DIFF_HEADS, ATT_NEAR),
        in_specs=[pl.BlockSpec(memory_space=pltpu.SMEM),
                  pl.BlockSpec((1, ATT_TQ, ATT_TK), lambda hi, pi: (pi, 0, 0))],
        out_specs=pl.BlockSpec((1, 1, ATT_TQ, ATT_TK), lambda hi, pi: (hi, pi, 0, 0)),
        out_shape=jax.ShapeDtypeStruct((DIFF_HEADS, ATT_NEAR, ATT_TQ, ATT_TK), F32),
        compiler_params=_cparams(("parallel", "parallel")),
        name="bias_tiles",
    )(rel_bias, jnp.asarray(pats))


def _diff_attn_kernel(q_ref, k_ref, v_ref, bias_ref, z_ref, lq1_ref, lk1_ref, lq2_ref, lk2_ref,
                      gain_ref, o_ref, s_ref, acc_ref, *, lam_init):
    tq, tk, u = ATT_TQ, ATT_TK, ATT_UNROLL
    r = tq // tk
    qi = pl.program_id(2)
    q = q_ref[0]
    qs = (q[:, :DIFF_DK], q[:, DIFF_DK:])

    def key_rows(c0, n):
        return pl.ds(pl.multiple_of(c0 * tk, tk), n * tk)

    def logits(c0, n):
        kt = k_ref[0, key_rows(c0, n), :]
        return [_mm_nt(qs[m], kt[:, m * DIFF_DK:(m + 1) * DIFF_DK]) for m in range(2)]

    def fold_max(mx, s):
        for j in range(s.shape[1] // LANE):
            mx = jnp.maximum(mx, s[:, j * LANE:(j + 1) * LANE])
        return mx

    def far_body(n, base):
        def body(i, mx):
            c0 = base + i * n
            out = []
            for m, s in enumerate(logits(c0, n)):
                for j in range(n):
                    s_ref[c0 + j, m] = s[:, j * tk:(j + 1) * tk]
                out.append(fold_max(mx[m], s))
            return tuple(out)
        return body

    def near(c0, j0, mx):
        out = []
        for m, s in enumerate(logits(c0, ATT_NEAR - j0)):
            acc = mx[m]
            for j in range(ATT_NEAR - j0):
                sj = s[:, j * tk:(j + 1) * tk] + bias_ref[0, j0 + j]
                s_ref[c0 + j, m] = sj
                acc = fold_max(acc, sj)
            out.append(acc)
        return tuple(out)

    n_far = jnp.maximum(r * qi - 1, 0)
    trips = n_far // u
    mx = (jnp.full((tq, LANE), NEG_MASK, F32),) * 2
    mx = lax.fori_loop(0, trips, far_body(u, 0), mx)
    mx = lax.fori_loop(0, n_far - trips * u, far_body(1, trips * u), mx)
    mx = lax.cond(qi >= 1, lambda v: near(n_far, 0, v), lambda v: near(0, 1, v), mx)
    mrep = [jnp.broadcast_to(jnp.max(v, axis=-1, keepdims=True), (tq, LANE)) for v in mx]

    acc_ref[...] = jnp.zeros_like(acc_ref)

    def pv_body(n, base):
        def body(i, lsum):
            c0 = base + i * n
            vt = v_ref[0, key_rows(c0, n), :]
            out = []
            for m in range(2):
                ps = []
                acc = lsum[m]
                for j in range(n):
                    s = s_ref[c0 + j, m]
                    for jj in range(tk // LANE):
                        p = jnp.exp2(s[:, jj * LANE:(jj + 1) * LANE] - mrep[m])
                        acc = acc + p
                        ps.append(p.astype(BF16))
                out.append(acc)
                acc_ref[m] += jnp.dot(jnp.concatenate(ps, axis=-1), vt, preferred_element_type=F32)
            return tuple(out)
        return body

    n_vis = r * (qi + 1)
    trips = n_vis // u
    lsum = (jnp.zeros((tq, LANE), F32),) * 2
    lsum = lax.fori_loop(0, trips, pv_body(u, 0), lsum)
    lsum = lax.fori_loop(0, n_vis - trips * u, pv_body(1, trips * u), lsum)
    l1, l2 = [jnp.sum(v, axis=-1, keepdims=True) for v in lsum]

    lam = (jnp.exp(jnp.sum(lq1_ref[...] * lk1_ref[...], axis=-1, keepdims=True))
           - jnp.exp(jnp.sum(lq2_ref[...] * lk2_ref[...], axis=-1, keepdims=True)) + lam_init)
    o = acc_ref[0] / l1 - lam * (acc_ref[1] / l2)
    o = o * lax.rsqrt(jnp.mean(o * o, axis=-1, keepdims=True) + EPS) * gain_ref[...]
    o = o * (1.0 - lam_init)
    o_ref[0] = (o * _silu(z_ref[0])).astype(o_ref.dtype)


def _diff_attention(qn, kn, vb, p_main, bias_tiles, lq1, lk1, lq2, lk2, subln_gain, lam_init):
    b, s, _ = qn.shape
    tq, tk = ATT_TQ, ATT_TK
    hw = 2 * DIFF_DK
    zb = (COL_DIFF + 3 * BRANCH_W) // DIFF_DV
    vec = pl.BlockSpec((1, DIFF_DK), lambda bi, hi, qi: (0, 0))
    r = lambda x: x.reshape(1, -1)
    return pl.pallas_call(
        functools.partial(_diff_attn_kernel, lam_init=lam_init),
        grid=(b, DIFF_HEADS, s // tq),
        in_specs=[pl.BlockSpec((1, tq, hw), lambda bi, hi, qi: (bi, qi, hi)),
                  pl.BlockSpec((1, s, hw), lambda bi, hi, qi: (bi, 0, hi)),
                  pl.BlockSpec((1, s, DIFF_DV), lambda bi, hi, qi: (bi, 0, hi)),
                  pl.BlockSpec((1, ATT_NEAR, tq, tk), lambda bi, hi, qi: (hi, 0, 0, 0)),
                  pl.BlockSpec((1, tq, DIFF_DV), lambda bi, hi, qi: (bi, qi, zb + hi)),
                  vec, vec, vec, vec,
                  pl.BlockSpec((1, DIFF_DV), lambda bi, hi, qi: (0, 0))],
        out_specs=pl.BlockSpec((1, tq, DIFF_DV), lambda bi, hi, qi: (bi, qi, hi)),
        out_shape=jax.ShapeDtypeStruct((b, s, BRANCH_W), BF16),
        scratch_shapes=[pltpu.VMEM((s // tk, 2, tq, tk), F32), pltpu.VMEM((2, tq, DIFF_DV), F32)],
        compiler_params=_cparams(("parallel", "parallel", "arbitrary")),
        name="diff_attention",
    )(qn, kn, vb, bias_tiles, p_main, r(lq1), r(lk1), r(lq2), r(lk2), r(subln_gain))


def _merge_kernel(y0_ref, y1_ref, y2_ref, w0_ref, w1_ref, w2_ref, g0_ref, g1_ref, g2_ref, o_ref):
    acc = None
    for y_ref, w_ref, g_ref in ((y0_ref, w0_ref, g0_ref), (y1_ref, w1_ref, g1_ref), (y2_ref, w2_ref, g2_ref)):
        term = jax.nn.sigmoid(g_ref[...]) * _dot(y_ref[...], w_ref[...])
        acc = term if acc is None else acc + term
    o_ref[...] = acc.astype(o_ref.dtype)


def _merge(ys, w_branch, layer, p_main2d):
    t = ys[0].shape[0]
    tm, tn = 256, 1024
    gate_base = COL_GATE // tn

    def wspec(n):
        return pl.BlockSpec((None, None, BRANCH_W, tn), lambda j, i: (layer, n, 0, j))

    def gspec(n):
        return pl.BlockSpec((tm, tn), lambda j, i: (i, gate_base + n * (D_MODEL // tn) + j))

    yspec = pl.BlockSpec((tm, BRANCH_W), lambda j, i: (i, 0))
    return pl.pallas_call(
        _merge_kernel,
        grid=(D_MODEL // tn, t // tm),
        in_specs=[yspec, yspec, yspec, wspec(0), wspec(1), wspec(2), gspec(0), gspec(1), gspec(2)],
        out_specs=pl.BlockSpec((tm, tn), lambda j, i: (i, j)),
        out_shape=jax.ShapeDtypeStruct((t, D_MODEL), BF16),
        compiler_params=_cparams(("parallel", "arbitrary")),
        name="merge",
    )(*ys, w_branch, w_branch, w_branch, p_main2d, p_main2d, p_main2d)


def _out_proj_kernel(a_ref, w_ref, x_ref, o_ref):
    o_ref[...] = x_ref[...] + _dot(a_ref[...], w_ref[...])


def _out_proj(merged, w_out, layer, x2d):
    t = merged.shape[0]
    tm, tn = 512, 1024
    return pl.pallas_call(
        _out_proj_kernel,
        grid=(D_MODEL // tn, t // tm),
        in_specs=[pl.BlockSpec((tm, D_MODEL), lambda j, i: (i, 0)),
                  pl.BlockSpec((None, D_MODEL, tn), lambda j, i: (layer, 0, j)),
                  pl.BlockSpec((tm, tn), lambda j, i: (i, j))],
        out_specs=pl.BlockSpec((tm, tn), lambda j, i: (i, j)),
        out_shape=jax.ShapeDtypeStruct((t, D_MODEL), F32),
        compiler_params=_cparams(("parallel", "arbitrary")),
        name="out_proj",
    )(merged, w_out, x2d)


def _rotary_tables(s):
    half = RET_DK // 2
    inv = ROPE_BASE ** (-np.arange(half, dtype=np.float64) / half)
    ang = np.arange(s, dtype=np.float64)[:, None] * inv[None, :]
    return jnp.asarray(np.cos(ang), F32), jnp.asarray(np.sin(ang), F32)


def _layer(x, layer, w_in, w_main, w_branch, w_out, norm_gain, ret_gn_gain, gdn_conv_w, gdn_a_log,
           gdn_dt_bias, gdn_norm_gain, diff_q_gain, diff_k_gain, lq1, lk1, lq2, lk2, diff_subln_gain,
           bias_tiles, log_gamma, cos, sin):
    b, s, d = x.shape
    t = b * s
    x2d = x.reshape(t, d)

    h = _rmsnorm(x2d, norm_gain)
    p_main2d = _matmul(h, w_main, layer, 0, MAIN_COLS, F32, 512, 1024, "in_proj")
    p_ab = _matmul(h, w_in, layer, AB_START // LANE, LANE, F32, 512, LANE, "in_proj_ab")
    p_main = p_main2d.reshape(b, s, MAIN_COLS)

    y_ret = _retention(p_main, ret_gn_gain, log_gamma, cos, sin)
    gb, gbt = _gdn_gates(p_ab.reshape(b, s, LANE), gdn_a_log, gdn_dt_bias)
    y_gdn = _gdn(p_main, gdn_conv_w, gb, gbt, gdn_norm_gain)
    qn, kn, vb = _qknorm(p_main, diff_q_gain, diff_k_gain)
    lam_init = 0.8 - 0.6 * math.exp(-0.3 * layer)
    y_diff = _diff_attention(qn, kn, vb, p_main, bias_tiles, lq1, lk1, lq2, lk2,
                             diff_subln_gain, lam_init)

    ys = [y.reshape(t, BRANCH_W) for y in (y_ret, y_gdn, y_diff)]
    merged = _merge(ys, w_branch, layer, p_main2d)
    return _out_proj(merged, w_out, layer, x2d).reshape(b, s, d)


def kernel(x, norm_gain, w_in, ret_gn_gain, gdn_conv_w, gdn_a_log, gdn_dt_bias, gdn_norm_gain,
           diff_q_gain, diff_k_gain, diff_lambda_q1, diff_lambda_k1, diff_lambda_q2, diff_lambda_k2,
           diff_subln_gain, rel_bias, w_branch, w_out):
    depth = w_in.shape[0]
    s = x.shape[1]
    log_gamma = jnp.asarray(np.log(1.0 - 2.0 ** (-5.0 - np.arange(RET_HEADS, dtype=np.float64))), F32)
    cos, sin = _rotary_tables(s)
    bias_tiles = _bias_tiles(rel_bias)
    w_main = jnp.concatenate([w_in[:, :, :AB_START], w_in[:, :, AB_START + AB_COLS:]], axis=2).astype(BF16)
    w_branch_b = w_branch.astype(BF16)
    w_out_b = w_out.astype(BF16)
    for l in range(depth):
        x = _layer(x, l, w_in, w_main, w_branch_b, w_out_b, norm_gain[l], ret_gn_gain[l], gdn_conv_w[l],
                   gdn_a_log[l], gdn_dt_bias[l], gdn_norm_gain[l], diff_q_gain[l], diff_k_gain[l],
                   diff_lambda_q1[l], diff_lambda_k1[l], diff_lambda_q2[l], diff_lambda_k2[l],
                   diff_subln_gain[l], bias_tiles, log_gamma, cos, sin)
    return x
```

```python
import functools
import math

import numpy as np
import jax
import jax.numpy as jnp
from jax import lax
from jax.experimental import pallas as pl
from jax.experimental.pallas import tpu as pltpu

F32 = jnp.float32
BF16 = jnp.bfloat16

D_MODEL = 4096
EPS = 1e-6
BRANCH_W = 2048
N_BRANCH = 3

RET_HEADS = 8
RET_DK = 256
RET_DV = BRANCH_W // RET_HEADS
ROPE_BASE = 10000.0

GDN_HEADS = 16
GDN_DK = 128
GDN_DV = BRANCH_W // GDN_HEADS
CONV_W = 4

DIFF_HEADS = 8
DIFF_DK = 128
DIFF_DV = BRANCH_W // DIFF_HEADS
MASK_CHUNK = 64
NUM_BUCKETS = 32
MAX_DISTANCE = 128

AB_START = 16384
AB_COLS = 2 * GDN_HEADS
P1_COLS = 16384
P2_COLS = 20480
COL_RET = 0
COL_GDN = 8192
COL_DIFF = 0
COL_GATE = 8192

LANE = 128
SUBLANE = 8
SEQ_CHUNK = 256
NEG_MASK = -1e30
LOG2E = math.log2(math.e)
GDN_HB = 4
ATT_TQ = 512
ATT_TK = 256
ATT_UNROLL = 2
ATT_NEAR = ATT_TQ // ATT_TK + 1
VMEM_LIMIT = 56 * 1024 * 1024


def _cparams(sem):
    return pltpu.CompilerParams(dimension_semantics=sem, vmem_limit_bytes=VMEM_LIMIT)


def _dot(a, b):
    return jnp.dot(a, b, preferred_element_type=F32)


def _mm(a, b):
    return _dot(a.astype(BF16), b.astype(BF16))


def _mm_nt(a, b):
    return lax.dot_general(a.astype(BF16), b.astype(BF16), (((1,), (1,)), ((), ())),
                           preferred_element_type=F32)


def _silu(x):
    return x * jax.nn.sigmoid(x)


def _rmsnorm_kernel(x_ref, g_ref, o_ref):
    x = x_ref[...]
    ms = jnp.mean(x * x, axis=-1, keepdims=True)
    o_ref[...] = (x * lax.rsqrt(ms + EPS) * g_ref[...]).astype(o_ref.dtype)


def _rmsnorm(x2d, gain):
    t, d = x2d.shape
    tm = 256
    return pl.pallas_call(
        _rmsnorm_kernel,
        grid=(t // tm,),
        in_specs=[pl.BlockSpec((tm, d), lambda i: (i, 0)),
                  pl.BlockSpec((1, d), lambda i: (0, 0))],
        out_specs=pl.BlockSpec((tm, d), lambda i: (i, 0)),
        out_shape=jax.ShapeDtypeStruct((t, d), BF16),
        compiler_params=_cparams(("parallel",)),
        name="rmsnorm",
    )(x2d, gain.reshape(1, d))


def _matmul_kernel(a_ref, w_ref, o_ref):
    o_ref[...] = _dot(a_ref[...], w_ref[...].astype(BF16)).astype(o_ref.dtype)


def _matmul(a, w, layer, col_block, n, out_dtype, tm, tn, name):
    m, k = a.shape
    return pl.pallas_call(
        _matmul_kernel,
        grid=(n // tn, m // tm),
        in_specs=[pl.BlockSpec((tm, k), lambda j, i: (i, 0)),
                  pl.BlockSpec((None, k, tn), lambda j, i: (layer, 0, col_block + j))],
        out_specs=pl.BlockSpec((tm, tn), lambda j, i: (i, j)),
        out_shape=jax.ShapeDtypeStruct((m, n), out_dtype),
        compiler_params=_cparams(("parallel", "arbitrary")),
        name=name,
    )(a, w)


def _retention_kernel(lg_ref, q_ref, k_ref, v_ref, z_ref, cos_ref, sin_ref, gain_ref,
                      o_ref, state_ref):
    c = SEQ_CHUNK
    h = pl.program_id(1)

    @pl.when(pl.program_id(2) == 0)
    def _():
        state_ref[...] = jnp.zeros_like(state_ref)

    lg = lg_ref[h]
    cos = cos_ref[...]
    sin = sin_ref[...]
    half = RET_DK // 2

    def rot(x):
        x1 = x[:, :half]
        x2 = x[:, half:]
        return jnp.concatenate([x1 * cos - x2 * sin, x1 * sin + x2 * cos], axis=-1)

    q = rot(q_ref[0])
    k = rot(k_ref[0]) * (RET_DK ** -0.5)
    v = v_ref[0]

    ri = lax.broadcasted_iota(jnp.int32, (c, c), 0)
    ci = lax.broadcasted_iota(jnp.int32, (c, c), 1)
    rel = (ri - ci).astype(F32)
    dmat = jnp.where(rel >= 0, jnp.exp(jnp.maximum(rel, 0.0) * lg), 0.0)
    idx = lax.broadcasted_iota(jnp.int32, (c, 1), 0).astype(F32)
    q_decay = jnp.exp((idx + 1.0) * lg)
    k_decay = jnp.exp((c - 1.0 - idx) * lg)
    chunk_decay = jnp.exp(jnp.full((1, 1), float(c), F32) * lg)

    scores = _mm_nt(q, k) * dmat
    intra = _mm(scores, v)
    state = state_ref[...]
    inter = _mm(q, state) * q_decay
    state_ref[...] = state * chunk_decay + _mm((k * k_decay).T, v)

    o = intra + inter
    mu = jnp.mean(o, axis=-1, keepdims=True)
    var = jnp.mean(jnp.square(o - mu), axis=-1, keepdims=True)
    o = (o - mu) * lax.rsqrt(var + EPS) * gain_ref[...]
    o_ref[0] = (o * _silu(z_ref[0])).astype(o_ref.dtype)


def _retention(p_main, gn_gain, log_gamma, cos, sin):
    b, s, _ = p_main.shape
    c = SEQ_CHUNK
    qb = COL_RET // RET_DK
    kb = qb + RET_HEADS
    vb = kb + RET_HEADS
    zb = vb + RET_HEADS

    def col(base):
        return pl.BlockSpec((1, c, RET_DK), lambda bi, hi, ci: (bi, ci, base + hi))

    return pl.pallas_call(
        _retention_kernel,
        grid=(b, RET_HEADS, s // c),
        in_specs=[pl.BlockSpec(memory_space=pltpu.SMEM),
                  col(qb), col(kb), col(vb), col(zb),
                  pl.BlockSpec((c, RET_DK // 2), lambda bi, hi, ci: (ci, 0)),
                  pl.BlockSpec((c, RET_DK // 2), lambda bi, hi, ci: (ci, 0)),
                  pl.BlockSpec((1, RET_DV), lambda bi, hi, ci: (0, hi))],
        out_specs=pl.BlockSpec((1, c, RET_DV), lambda bi, hi, ci: (bi, ci, hi)),
        out_shape=jax.ShapeDtypeStruct((b, s, BRANCH_W), BF16),
        scratch_shapes=[pltpu.VMEM((RET_DK, RET_DV), F32)],
        compiler_params=_cparams(("parallel", "parallel", "arbitrary")),
        name="retention",
    )(log_gamma, p_main, p_main, p_main, p_main, cos, sin, gn_gain.reshape(1, BRANCH_W))


def _gdn_gate_kernel(ab_ref, alog_ref, dt_ref, gb_ref, gbt_ref):
    c = SEQ_CHUNK
    x = ab_ref[0]
    sp = jnp.maximum(x + dt_ref[...], 0.0) + jnp.log1p(jnp.exp(-jnp.abs(x + dt_ref[...])))
    g = -jnp.exp(alog_ref[...]) * sp
    ri = lax.broadcasted_iota(jnp.int32, (c, c), 0)
    ci = lax.broadcasted_iota(jnp.int32, (c, c), 1)
    tri = jnp.where(ri >= ci, 1.0, 0.0).astype(F32)
    gc = jnp.dot(tri, g, preferred_element_type=F32, precision=lax.Precision.HIGHEST)
    lane = lax.broadcasted_iota(jnp.int32, x.shape, 1)
    out = jnp.where(lane < GDN_HEADS, gc, jax.nn.sigmoid(x))
    gb_ref[0] = out
    gbt_ref[0] = out.T


def _gdn_gates(p_ab, a_log, dt_bias):
    b, s, _ = p_ab.shape
    c = SEQ_CHUNK
    pad = LANE - GDN_HEADS
    alog = jnp.pad(a_log, (0, pad)).reshape(1, LANE)
    dt = jnp.pad(dt_bias, (0, pad)).reshape(1, LANE)
    return pl.pallas_call(
        _gdn_gate_kernel,
        grid=(b, s // c),
        in_specs=[pl.BlockSpec((1, c, LANE), lambda bi, ci: (bi, ci, 0)),
                  pl.BlockSpec((1, LANE), lambda bi, ci: (0, 0)),
                  pl.BlockSpec((1, LANE), lambda bi, ci: (0, 0))],
        out_specs=[pl.BlockSpec((1, c, LANE), lambda bi, ci: (bi, ci, 0)),
                   pl.BlockSpec((1, LANE, c), lambda bi, ci: (bi, 0, ci))],
        out_shape=[jax.ShapeDtypeStruct((b, s, LANE), F32),
                   jax.ShapeDtypeStruct((b, LANE, s), F32)],
        compiler_params=_cparams(("parallel", "parallel")),
        name="gdn_gates",
    )(p_ab, alog, dt)


def _block_levels(c):
    idx = np.arange(c)
    x = idx[:, None] ^ idx[None, :]
    return np.where(x > 0, np.floor(np.log2(np.maximum(x, 1))) + 1, 0).astype(np.float32)


def _unit_lower_inverse(a, lvl, eye):
    n = range(len(a))
    ab = [x.astype(BF16) for x in a]
    zero = jnp.zeros_like(ab[0])
    base_shift = 3
    a8 = [jnp.where(lvl <= base_shift, ab[i], zero) for i in n]
    t = [eye - a8[i].astype(F32) for i in n]
    a2 = [_dot(a8[i], a8[i]) for i in n]
    t = [t[i] + _mm(t[i], a2[i]) for i in n]
    a4 = [_mm(a2[i], a2[i]) for i in n]
    tb = [(t[i] + _mm(t[i], a4[i])).astype(BF16) for i in n]
    shift = base_shift
    while (1 << shift) < a[0].shape[0]:
        off = [jnp.where(lvl == shift + 1, ab[i], zero) for i in n]
        x = [_dot(tb[i], off[i]).astype(BF16) for i in n]
        tb = [tb[i] - _dot(x[i], tb[i]).astype(BF16) for i in n]
        shift += 1
    return tb


def _gdn_kernel(q_ref, k_ref, v_ref, z_ref, qh_ref, kh_ref, vh_ref, wq_ref, wk_ref, wv_ref,
                gb_ref, gbt_ref, lvl_ref, gain_ref, o_ref, state_ref):
    c = SEQ_CHUNK
    first = pl.program_id(2) == 0

    @pl.when(first)
    def _():
        state_ref[...] = jnp.zeros_like(state_ref)

    gb = gb_ref[0]
    lane = lax.broadcasted_iota(jnp.int32, gb.shape, 1)
    ri = lax.broadcasted_iota(jnp.int32, (c, c), 0)
    ci = lax.broadcasted_iota(jnp.int32, (c, c), 1)
    eye = jnp.where(ri == ci, 1.0, 0.0).astype(F32)
    lvl = lvl_ref[...]

    def conv_silu(x_ref, halo_ref, w_ref, sl):
        halo = jnp.where(first, 0.0, halo_ref[0, :, sl])
        xx = jnp.concatenate([halo, x_ref[0, :, sl]], axis=0)
        w = w_ref[:, sl]
        y = xx[SUBLANE:SUBLANE + c] * w[CONV_W - 1:CONV_W]
        for i in range(CONV_W - 1):
            off = SUBLANE - (CONV_W - 1) + i
            y = y + xx[off:off + c] * w[i:i + 1]
        return _silu(y)

    heads = range(GDN_HB)
    sls = [slice(j * GDN_DK, (j + 1) * GDN_DK) for j in heads]
    hs = [pl.program_id(1) * GDN_HB + j for j in heads]
    q = [conv_silu(q_ref, qh_ref, wq_ref, sl) for sl in sls]
    k = [conv_silu(k_ref, kh_ref, wk_ref, sl) for sl in sls]
    v = [conv_silu(v_ref, vh_ref, wv_ref, sl) for sl in sls]
    q = [x * lax.rsqrt(jnp.sum(x * x, axis=-1, keepdims=True) + EPS) * (GDN_DK ** -0.5) for x in q]
    k = [x * lax.rsqrt(jnp.sum(x * x, axis=-1, keepdims=True) + EPS) for x in k]

    gc_col = [jnp.sum(jnp.where(lane == h, gb, 0.0), axis=1, keepdims=True) for h in hs]
    beta = [jnp.sum(jnp.where(lane == h + GDN_HEADS, gb, 0.0), axis=1, keepdims=True) for h in hs]
    gc_row = [gbt_ref[0, pl.ds(h, 1), :] for h in hs]
    g_last = [x[c - 1:c, :] for x in gc_col]
    decay = [jnp.where(ri >= ci, jnp.exp(jnp.minimum(gc_col[j] - gc_row[j], 0.0)), 0.0) for j in heads]

    kb = [x.astype(BF16) for x in k]
    k_beta = [k[j] * beta[j] for j in heads]
    kk = [_mm_nt(k_beta[j], kb[j]) for j in heads]
    a_mat = [jnp.where(ri > ci, kk[j] * decay[j], 0.0) for j in heads]
    t_inv = _unit_lower_inverse(a_mat, lvl, eye)
    e_gc = [jnp.exp(x) for x in gc_col]
    rhs = [jnp.concatenate([v[j] * beta[j], k_beta[j] * e_gc[j]], axis=-1).astype(BF16) for j in heads]
    sol = [_dot(t_inv[j], rhs[j]) for j in heads]
    attn = [_mm_nt(q[j], kb[j]) * decay[j] for j in heads]
    q_g = [q[j] * e_gc[j] for j in heads]
    k_g = [k[j] * jnp.exp(g_last[j] - gc_col[j]) for j in heads]

    state = [state_ref[j] for j in heads]
    both = [_mm(jnp.concatenate([sol[j][:, GDN_DV:], q_g[j]], axis=0), state[j]) for j in heads]
    v_new = [(sol[j][:, :GDN_DV] - both[j][:c]).astype(BF16) for j in heads]
    o = [both[j][c:] + _mm(attn[j], v_new[j]) for j in heads]
    for j in heads:
        state_ref[j] = state[j] * jnp.exp(g_last[j]) + _mm(k_g[j].T, v_new[j])
    for j in heads:
        y = o[j] * lax.rsqrt(jnp.mean(o[j] * o[j], axis=-1, keepdims=True) + EPS) * gain_ref[...]
        o_ref[0, :, sls[j]] = (y * _silu(z_ref[0, :, sls[j]])).astype(o_ref.dtype)


def _gdn(p_main, conv_w, gb, gbt, norm_gain):
    b, s, _ = p_main.shape
    c = SEQ_CHUNK
    w = GDN_HB * GDN_DK
    groups = GDN_HEADS // GDN_HB
    qb = COL_GDN // w
    kb = qb + groups
    vb = kb + groups
    zb = vb + groups
    rows_per_chunk = c // SUBLANE

    def col(base):
        return pl.BlockSpec((1, c, w), lambda bi, hi, ci: (bi, ci, base + hi))

    def halo(base):
        return pl.BlockSpec((1, SUBLANE, w),
                            lambda bi, hi, ci: (bi, jnp.maximum(ci * rows_per_chunk - 1, 0), base + hi))

    def convw(base):
        return pl.BlockSpec((CONV_W, w), lambda bi, hi, ci: (0, base + hi))

    lvl = jnp.asarray(_block_levels(c), BF16)
    return pl.pallas_call(
        _gdn_kernel,
        grid=(b, groups, s // c),
        in_specs=[col(qb), col(kb), col(vb), col(zb),
                  halo(qb), halo(kb), halo(vb),
                  convw(0), convw(groups), convw(2 * groups),
                  pl.BlockSpec((1, c, LANE), lambda bi, hi, ci: (bi, ci, 0)),
                  pl.BlockSpec((1, LANE, c), lambda bi, hi, ci: (bi, 0, ci)),
                  pl.BlockSpec((c, c), lambda bi, hi, ci: (0, 0)),
                  pl.BlockSpec((1, GDN_DV), lambda bi, hi, ci: (0, 0))],
        out_specs=pl.BlockSpec((1, c, w), lambda bi, hi, ci: (bi, ci, hi)),
        out_shape=jax.ShapeDtypeStruct((b, s, BRANCH_W), BF16),
        scratch_shapes=[pltpu.VMEM((GDN_HB, GDN_DK, GDN_DV), F32)],
        compiler_params=_cparams(("parallel", "parallel", "arbitrary")),
        name="gdn",
    )(p_main, p_main, p_main, p_main, p_main, p_main, p_main, conv_w, conv_w, conv_w,
      gb, gbt, lvl, norm_gain.reshape(1, GDN_DV))


def _qknorm_kernel(q_ref, k_ref, v_ref, qg_ref, kg_ref, qo_ref, ko_ref, vo_ref):
    def norm(x_ref, g_ref, o_ref, scale):
        g = g_ref[...]
        for i in range(BRANCH_W // DIFF_DK):
            x = x_ref[0, :, i * DIFF_DK:(i + 1) * DIFF_DK]
            y = x * lax.rsqrt(jnp.mean(x * x, axis=-1, keepdims=True) + EPS) * g
            o_ref[0, :, i * DIFF_DK:(i + 1) * DIFF_DK] = (y * scale).astype(o_ref.dtype)

    norm(q_ref, qg_ref, qo_ref, DIFF_DK ** -0.5 * LOG2E)
    norm(k_ref, kg_ref, ko_ref, 1.0)
    vo_ref[...] = v_ref[...].astype(vo_ref.dtype)


def _qknorm(p_main, q_gain, k_gain):
    b, s, _ = p_main.shape
    c = SEQ_CHUNK
    base = COL_DIFF // BRANCH_W

    def col(j):
        return pl.BlockSpec((1, c, BRANCH_W), lambda bi, ci: (bi, ci, base + j))

    out = pl.BlockSpec((1, c, BRANCH_W), lambda bi, ci: (bi, ci, 0))
    gain = pl.BlockSpec((1, DIFF_DK), lambda bi, ci: (0, 0))
    shape = jax.ShapeDtypeStruct((b, s, BRANCH_W), BF16)
    return pl.pallas_call(
        _qknorm_kernel,
        grid=(b, s // c),
        in_specs=[col(0), col(1), col(2), gain, gain],
        out_specs=[out, out, out],
        out_shape=[shape, shape, shape],
        compiler_params=_cparams(("parallel", "parallel")),
        name="qknorm",
    )(p_main, p_main, p_main, q_gain.reshape(1, DIFF_DK), k_gain.reshape(1, DIFF_DK))


def _rel_bucket_np(rel):
    nb = NUM_BUCKETS // 2
    max_exact = nb // 2
    n = np.abs(rel)
    nf = np.maximum(n, 1).astype(np.float64)
    large = max_exact + (np.log(nf / max_exact) / math.log(MAX_DISTANCE / max_exact)
                         * (nb - max_exact)).astype(np.int32)
    large = np.minimum(large, nb - 1)
    return np.where(rel > 0, nb, 0) + np.where(n < max_exact, n, large)


FAR_BUCKET = NUM_BUCKETS // 2 - 1


def _bias_patterns():
    qpos = np.arange(ATT_TQ)[:, None]
    pats = []
    for j in range(ATT_NEAR):
        kpos = np.arange(ATT_TK)[None, :] + (j - 1) * ATT_TK
        bucket = _rel_bucket_np(kpos - qpos)
        visible = (kpos // MASK_CHUNK) <= (qpos // MASK_CHUNK)
        pats.append(np.where(visible, bucket, -1))
    return np.stack(pats).astype(np.int32)


def _bias_tile_kernel(rb_ref, pat_ref, o_ref):
    h = pl.program_id(0)
    pat = pat_ref[0]
    far = rb_ref[FAR_BUCKET, h]
    acc = jnp.full(pat.shape, NEG_MASK, F32)
    for bkt in range(NUM_BUCKETS):
        acc = jnp.where(pat == bkt, (rb_ref[bkt, h] - far) * LOG2E, acc)
    o_ref[0, 0] = acc


def _bias_tiles(rel_bias):
    pats = _bias_patterns()
    return pl.pallas_call(
        _bias_tile_kernel,
        grid=(DIFF_HEADS, ATT_NEAR),
        in_specs=[pl.BlockSpec(memory_space=pltpu.SMEM),
                  pl.BlockSpec((1, ATT_TQ, ATT_TK), lambda hi, pi: (pi, 0, 0))],
        out_specs=pl.BlockSpec((1, 1, ATT_TQ, ATT_TK), lambda hi, pi: (hi, pi, 0, 0)),
        out_shape=jax.ShapeDtypeStruct((DIFF_HEADS, ATT_NEAR, ATT_TQ, ATT_TK), F32),
        compiler_params=_cparams(("parallel", "parallel")),
        name="bias_tiles",
    )(rel_bias, jnp.asarray(pats))


def _diff_attn_kernel(q_ref, k_ref, v_ref, bias_ref, z_ref, lq1_ref, lk1_ref, lq2_ref, lk2_ref,
                      gain_ref, o_ref, s_ref, acc_ref, *, lam_init):
    tq, tk, u = ATT_TQ, ATT_TK, ATT_UNROLL
    r = tq // tk
    qi = pl.program_id(2)
    q = q_ref[0]
    qs = (q[:, :DIFF_DK], q[:, DIFF_DK:])

    def key_rows(c0, n):
        return pl.ds(pl.multiple_of(c0 * tk, tk), n * tk)

    def logits(c0, n):
        kt = k_ref[0, key_rows(c0, n), :]
        return [_mm_nt(qs[m], kt[:, m * DIFF_DK:(m + 1) * DIFF_DK]) for m in range(2)]

    def fold_max(mx, s):
        for j in range(s.shape[1] // LANE):
            mx = jnp.maximum(mx, s[:, j * LANE:(j + 1) * LANE])
        return mx

    def far_body(n, base):
        def body(i, mx):
            c0 = base + i * n
            out = []
            for m, s in enumerate(logits(c0, n)):
                for j in range(n):
                    s_ref[c0 + j, m] = s[:, j * tk:(j + 1) * tk]
                out.append(fold_max(mx[m], s))
            return tuple(out)
        return body

    def near(c0, j0, mx):
        out = []
        for m, s in enumerate(logits(c0, ATT_NEAR - j0)):
            acc = mx[m]
            for j in range(ATT_NEAR - j0):
                sj = s[:, j * tk:(j + 1) * tk] + bias_ref[0, j0 + j]
                s_ref[c0 + j, m] = sj
                acc = fold_max(acc, sj)
            out.append(acc)
        return tuple(out)

    n_far = jnp.maximum(r * qi - 1, 0)
    trips = n_far // u
    mx = (jnp.full((tq, LANE), NEG_MASK, F32),) * 2
    mx = lax.fori_loop(0, trips, far_body(u, 0), mx)
    mx = lax.fori_loop(0, n_far - trips * u, far_body(1, trips * u), mx)
    mx = lax.cond(qi >= 1, lambda v: near(n_far, 0, v), lambda v: near(0, 1, v), mx)
    mrep = [jnp.broadcast_to(jnp.max(v, axis=-1, keepdims=True), (tq, LANE)) for v in mx]

    acc_ref[...] = jnp.zeros_like(acc_ref)

    def pv_body(n, base):
        def body(i, lsum):
            c0 = base + i * n
            vt = v_ref[0, key_rows(c0, n), :]
            out = []
            for m in range(2):
                ps = []
                acc = lsum[m]
                for j in range(n):
                    s = s_ref[c0 + j, m]
                    for jj in range(tk // LANE):
                        p = jnp.exp2(s[:, jj * LANE:(jj + 1) * LANE] - mrep[m])
                        acc = acc + p
                        ps.append(p.astype(BF16))
                out.append(acc)
                acc_ref[m] += jnp.dot(jnp.concatenate(ps, axis=-1), vt, preferred_element_type=F32)
            return tuple(out)
        return body

    n_vis = r * (qi + 1)
    trips = n_vis // u
    lsum = (jnp.zeros((tq, LANE), F32),) * 2
    lsum = lax.fori_loop(0, trips, pv_body(u, 0), lsum)
    lsum = lax.fori_loop(0, n_vis - trips * u, pv_body(1, trips * u), lsum)
    l1, l2 = [jnp.sum(v, axis=-1, keepdims=True) for v in lsum]

    lam = (jnp.exp(jnp.sum(lq1_ref[...] * lk1_ref[...], axis=-1, keepdims=True))
           - jnp.exp(jnp.sum(lq2_ref[...] * lk2_ref[...], axis=-1, keepdims=True)) + lam_init)
    o = acc_ref[0] / l1 - lam * (acc_ref[1] / l2)
    o = o * lax.rsqrt(jnp.mean(o * o, axis=-1, keepdims=True) + EPS) * gain_ref[...]
    o = o * (1.0 - lam_init)
    o_ref[0] = (o * _silu(z_ref[0])).astype(o_ref.dtype)


def _diff_attention(qn, kn, vb, p_main, bias_tiles, lq1, lk1, lq2, lk2, subln_gain, lam_init):
    b, s, _ = qn.shape
    tq, tk = ATT_TQ, ATT_TK
    hw = 2 * DIFF_DK
    zb = (COL_DIFF + 3 * BRANCH_W) // DIFF_DV
    vec = pl.BlockSpec((1, DIFF_DK), lambda bi, hi, qi: (0, 0))
    r = lambda x: x.reshape(1, -1)
    return pl.pallas_call(
        functools.partial(_diff_attn_kernel, lam_init=lam_init),
        grid=(b, DIFF_HEADS, s // tq),
        in_specs=[pl.BlockSpec((1, tq, hw), lambda bi, hi, qi: (bi, qi, hi)),
                  pl.BlockSpec((1, s, hw), lambda bi, hi, qi: (bi, 0, hi)),
                  pl.BlockSpec((1, s, DIFF_DV), lambda bi, hi, qi: (bi, 0, hi)),
                  pl.BlockSpec((1, ATT_NEAR, tq, tk), lambda bi, hi, qi: (hi, 0, 0, 0)),
                  pl.BlockSpec((1, tq, DIFF_DV), lambda bi, hi, qi: (bi, qi, zb + hi)),
                  vec, vec, vec, vec,
                  pl.BlockSpec((1, DIFF_DV), lambda bi, hi, qi: (0, 0))],
        out_specs=pl.BlockSpec((1, tq, DIFF_DV), lambda bi, hi, qi: (bi, qi, hi)),
        out_shape=jax.ShapeDtypeStruct((b, s, BRANCH_W), BF16),
        scratch_shapes=[pltpu.VMEM((s // tk, 2, tq, tk), F32), pltpu.VMEM((2, tq, DIFF_DV), F32)],
        compiler_params=_cparams(("parallel", "parallel", "arbitrary")),
        name="diff_attention",
    )(qn, kn, vb, bias_tiles, p_main, r(lq1), r(lk1), r(lq2), r(lk2), r(subln_gain))


def _merge_kernel(y0_ref, y1_ref, y2_ref, w0_ref, w1_ref, w2_ref, g0_ref, g1_ref, g2_ref, o_ref):
    acc = None
    for y_ref, w_ref, g_ref in ((y0_ref, w0_ref, g0_ref), (y1_ref, w1_ref, g1_ref), (y2_ref, w2_ref, g2_ref)):
        term = jax.nn.sigmoid(g_ref[...]) * _dot(y_ref[...], w_ref[...])
        acc = term if acc is None else acc + term
    o_ref[...] = acc.astype(o_ref.dtype)


def _merge(ys, w_branch, layer, p_main2d):
    t = ys[0].shape[0]
    tm, tn = 256, 1024
    gate_base = COL_GATE // tn

    def wspec(n):
        return pl.BlockSpec((None, None, BRANCH_W, tn), lambda j, i: (layer, n, 0, j))

    def gspec(n):
        return pl.BlockSpec((tm, tn), lambda j, i: (i, gate_base + n * (D_MODEL // tn) + j))

    yspec = pl.BlockSpec((tm, BRANCH_W), lambda j, i: (i, 0))
    return pl.pallas_call(
        _merge_kernel,
        grid=(D_MODEL // tn, t // tm),
        in_specs=[yspec, yspec, yspec, wspec(0), wspec(1), wspec(2), gspec(0), gspec(1), gspec(2)],
        out_specs=pl.BlockSpec((tm, tn), lambda j, i: (i, j)),
        out_shape=jax.ShapeDtypeStruct((t, D_MODEL), BF16),
        compiler_params=_cparams(("parallel", "arbitrary")),
        name="merge",
    )(*ys, w_branch, w_branch, w_branch, p_main2d, p_main2d, p_main2d)


def _out_proj_kernel(a_ref, w_ref, x_ref, o_ref):
    o_ref[...] = x_ref[...] + _dot(a_ref[...], w_ref[...])


def _out_proj(merged, w_out, layer, x2d):
    t = merged.shape[0]
    tm, tn = 512, 1024
    return pl.pallas_call(
        _out_proj_kernel,
        grid=(D_MODEL // tn, t // tm),
        in_specs=[pl.BlockSpec((tm, D_MODEL), lambda j, i: (i, 0)),
                  pl.BlockSpec((None, D_MODEL, tn), lambda j, i: (layer, 0, j)),
                  pl.BlockSpec((tm, tn), lambda j, i: (i, j))],
        out_specs=pl.BlockSpec((tm, tn), lambda j, i: (i, j)),
        out_shape=jax.ShapeDtypeStruct((t, D_MODEL), F32),
        compiler_params=_cparams(("parallel", "arbitrary")),
        name="out_proj",
    )(merged, w_out, x2d)


def _rotary_tables(s):
    half = RET_DK // 2
    inv = ROPE_BASE ** (-np.arange(half, dtype=np.float64) / half)
    ang = np.arange(s, dtype=np.float64)[:, None] * inv[None, :]
    return jnp.asarray(np.cos(ang), F32), jnp.asarray(np.sin(ang), F32)


def _layer(x, layer, w1, w2, w_ab, w_branch, w_out, norm_gain, ret_gn_gain, gdn_conv_w, gdn_a_log,
           gdn_dt_bias, gdn_norm_gain, diff_q_gain, diff_k_gain, lq1, lk1, lq2, lk2, diff_subln_gain,
           bias_tiles, log_gamma, cos, sin):
    b, s, d = x.shape
    t = b * s
    x2d = x.reshape(t, d)

    h = _rmsnorm(x2d, norm_gain)
    p1_2d = _matmul(h, w1, layer, 0, P1_COLS, F32, 512, 1024, "in_proj_1")
    p2_2d = _matmul(h, w2, layer, 0, P2_COLS, F32, 512, 1024, "in_proj_2")
    p_ab = _matmul(h, w_ab, layer, 0, LANE, F32, 512, LANE, "in_proj_ab")
    p1 = p1_2d.reshape(b, s, P1_COLS)
    p2 = p2_2d.reshape(b, s, P2_COLS)

    y_ret = _retention(p1, ret_gn_gain, log_gamma, cos, sin)
    gb, gbt = _gdn_gates(p_ab.reshape(b, s, LANE), gdn_a_log, gdn_dt_bias)
    y_gdn = _gdn(p1, gdn_conv_w, gb, gbt, gdn_norm_gain)
    qn, kn, vb = _qknorm(p2, diff_q_gain, diff_k_gain)
    lam_init = 0.8 - 0.6 * math.exp(-0.3 * layer)
    y_diff = _diff_attention(qn, kn, vb, p2, bias_tiles, lq1, lk1, lq2, lk2,
                             diff_subln_gain, lam_init)

    ys = [y.reshape(t, BRANCH_W) for y in (y_ret, y_gdn, y_diff)]
    merged = _merge(ys, w_branch, layer, p2_2d)
    return _out_proj(merged, w_out, layer, x2d).reshape(b, s, d)


def kernel(x, norm_gain, w_in, ret_gn_gain, gdn_conv_w, gdn_a_log, gdn_dt_bias, gdn_norm_gain,
           diff_q_gain, diff_k_gain, diff_lambda_q1, diff_lambda_k1, diff_lambda_q2, diff_lambda_k2,
           diff_subln_gain, rel_bias, w_branch, w_out):
    depth = w_in.shape[0]
    s = x.shape[1]
    log_gamma = jnp.asarray(np.log(1.0 - 2.0 ** (-5.0 - np.arange(RET_HEADS, dtype=np.float64))), F32)
    cos, sin = _rotary_tables(s)
    bias_tiles = _bias_tiles(rel_bias)
    w1 = w_in[:, :, :AB_START].astype(BF16)
    w2 = w_in[:, :, AB_START + AB_COLS:].astype(BF16)
    w_ab = w_in[:, :, AB_START:AB_START + LANE].astype(BF16)
    w_branch_b = w_branch.astype(BF16)
    w_out_b = w_out.astype(BF16)
    for l in range(depth):
        x = _layer(x, l, w1, w2, w_ab, w_branch_b, w_out_b, norm_gain[l], ret_gn_gain[l], gdn_conv_w[l],
                   gdn_a_log[l], gdn_dt_bias[l], gdn_norm_gain[l], diff_q_gain[l], diff_k_gain[l],
                   diff_lambda_q1[l], diff_lambda_k1[l], diff_lambda_q2[l], diff_lambda_k2[l],
                   diff_subln_gain[l], bias_tiles, log_gamma, cos, sin)
    return x
```

```python
import functools
import math

import numpy as np
import jax
import jax.numpy as jnp
from jax import lax
from jax.experimental import pallas as pl
from jax.experimental.pallas import tpu as pltpu

F32 = jnp.float32
BF16 = jnp.bfloat16

D_MODEL = 4096
EPS = 1e-6
BRANCH_W = 2048
N_BRANCH = 3

RET_HEADS = 8
RET_DK = 256
RET_DV = BRANCH_W // RET_HEADS
ROPE_BASE = 10000.0

GDN_HEADS = 16
GDN_DK = 128
GDN_DV = BRANCH_W // GDN_HEADS
CONV_W = 4

DIFF_HEADS = 8
DIFF_DK = 128
DIFF_DV = BRANCH_W // DIFF_HEADS
MASK_CHUNK = 64
NUM_BUCKETS = 32
MAX_DISTANCE = 128

AB_START = 16384
AB_COLS = 2 * GDN_HEADS
P1_COLS = 16384
P2_COLS = 20480
COL_RET = 0
COL_GDN = 8192
COL_DIFF = 0
COL_GATE = 8192

LANE = 128
SUBLANE = 8
SEQ_CHUNK = 256
NEG_MASK = -1e30
LOG2E = math.log2(math.e)
RET_HB = 2
GDN_HB = 4
ATT_TQ = 512
ATT_TK = 256
ATT_UNROLL = 2
ATT_NEAR = ATT_TQ // ATT_TK + 1
VMEM_LIMIT = 56 * 1024 * 1024


def _cparams(sem):
    return pltpu.CompilerParams(dimension_semantics=sem, vmem_limit_bytes=VMEM_LIMIT)


def _dot(a, b):
    return jnp.dot(a, b, preferred_element_type=F32)


def _mm(a, b):
    return _dot(a.astype(BF16), b.astype(BF16))


def _mm_nt(a, b):
    return lax.dot_general(a.astype(BF16), b.astype(BF16), (((1,), (1,)), ((), ())),
                           preferred_element_type=F32)


def _silu(x):
    return x * jax.nn.sigmoid(x)


def _rmsnorm_kernel(x_ref, g_ref, o_ref):
    x = x_ref[...]
    ms = jnp.mean(x * x, axis=-1, keepdims=True)
    o_ref[...] = (x * lax.rsqrt(ms + EPS) * g_ref[...]).astype(o_ref.dtype)


def _rmsnorm(x2d, gain):
    t, d = x2d.shape
    tm = 256
    return pl.pallas_call(
        _rmsnorm_kernel,
        grid=(t // tm,),
        in_specs=[pl.BlockSpec((tm, d), lambda i: (i, 0)),
                  pl.BlockSpec((1, d), lambda i: (0, 0))],
        out_specs=pl.BlockSpec((tm, d), lambda i: (i, 0)),
        out_shape=jax.ShapeDtypeStruct((t, d), BF16),
        compiler_params=_cparams(("parallel",)),
        name="rmsnorm",
    )(x2d, gain.reshape(1, d))


def _matmul_kernel(a_ref, w_ref, o_ref):
    o_ref[...] = _dot(a_ref[...], w_ref[...].astype(BF16)).astype(o_ref.dtype)


def _matmul(a, w, layer, col_block, n, out_dtype, tm, tn, name):
    m, k = a.shape
    return pl.pallas_call(
        _matmul_kernel,
        grid=(n // tn, m // tm),
        in_specs=[pl.BlockSpec((tm, k), lambda j, i: (i, 0)),
                  pl.BlockSpec((None, k, tn), lambda j, i: (layer, 0, col_block + j))],
        out_specs=pl.BlockSpec((tm, tn), lambda j, i: (i, j)),
        out_shape=jax.ShapeDtypeStruct((m, n), out_dtype),
        compiler_params=_cparams(("parallel", "arbitrary")),
        name=name,
    )(a, w)


def _retention_kernel(lg_ref, q_ref, k_ref, v_ref, z_ref, cos_ref, sin_ref, gain_ref,
                      o_ref, state_ref):
    c = SEQ_CHUNK

    @pl.when(pl.program_id(2) == 0)
    def _():
        state_ref[...] = jnp.zeros_like(state_ref)

    cos = cos_ref[...]
    sin = sin_ref[...]
    half = RET_DK // 2

    def rot(x):
        x1 = x[:, :half]
        x2 = x[:, half:]
        return jnp.concatenate([x1 * cos - x2 * sin, x1 * sin + x2 * cos], axis=-1)

    heads = range(RET_HB)
    sls = [slice(j * RET_DK, (j + 1) * RET_DK) for j in heads]
    lg = [lg_ref[pl.program_id(1) * RET_HB + j] for j in heads]
    ri = lax.broadcasted_iota(jnp.int32, (c, c), 0)
    ci = lax.broadcasted_iota(jnp.int32, (c, c), 1)
    rel = (ri - ci).astype(F32)
    rel_pos = jnp.maximum(rel, 0.0)
    idx = lax.broadcasted_iota(jnp.int32, (c, 1), 0).astype(F32)

    q = [rot(q_ref[0, :, sl]).astype(BF16) for sl in sls]
    k = [rot(k_ref[0, :, sl]) * (RET_DK ** -0.5) for sl in sls]
    v = [v_ref[0, :, sl].astype(BF16) for sl in sls]
    dmat = [jnp.where(rel >= 0, jnp.exp(rel_pos * lg[j]), 0.0) for j in heads]
    q_decay = [jnp.exp((idx + 1.0) * lg[j]) for j in heads]
    k_decay = [jnp.exp((c - 1.0 - idx) * lg[j]) for j in heads]
    chunk_decay = [jnp.exp(jnp.full((1, 1), float(c), F32) * lg[j]) for j in heads]

    scores = [_mm_nt(q[j], k[j]) * dmat[j] for j in heads]
    state = [state_ref[j] for j in heads]
    inter = [_mm(q[j], state[j]) * q_decay[j] for j in heads]
    intra = [_mm(scores[j], v[j]) for j in heads]
    for j in heads:
        state_ref[j] = state[j] * chunk_decay[j] + _mm((k[j] * k_decay[j]).T, v[j])
    for j in heads:
        o = intra[j] + inter[j]
        mu = jnp.mean(o, axis=-1, keepdims=True)
        var = jnp.mean(jnp.square(o - mu), axis=-1, keepdims=True)
        o = (o - mu) * lax.rsqrt(var + EPS) * gain_ref[:, sls[j]]
        o_ref[0, :, sls[j]] = (o * _silu(z_ref[0, :, sls[j]])).astype(o_ref.dtype)


def _retention(p_main, gn_gain, log_gamma, cos, sin):
    b, s, _ = p_main.shape
    c = SEQ_CHUNK
    w = RET_HB * RET_DK
    groups = RET_HEADS // RET_HB
    qb = COL_RET // w
    kb = qb + groups
    vb = kb + groups
    zb = vb + groups

    def col(base):
        return pl.BlockSpec((1, c, w), lambda bi, hi, ci: (bi, ci, base + hi))

    return pl.pallas_call(
        _retention_kernel,
        grid=(b, groups, s // c),
        in_specs=[pl.BlockSpec(memory_space=pltpu.SMEM),
                  col(qb), col(kb), col(vb), col(zb),
                  pl.BlockSpec((c, RET_DK // 2), lambda bi, hi, ci: (ci, 0)),
                  pl.BlockSpec((c, RET_DK // 2), lambda bi, hi, ci: (ci, 0)),
                  pl.BlockSpec((1, w), lambda bi, hi, ci: (0, hi))],
        out_specs=pl.BlockSpec((1, c, w), lambda bi, hi, ci: (bi, ci, hi)),
        out_shape=jax.ShapeDtypeStruct((b, s, BRANCH_W), BF16),
        scratch_shapes=[pltpu.VMEM((RET_HB, RET_DK, RET_DV), F32)],
        compiler_params=_cparams(("parallel", "parallel", "arbitrary")),
        name="retention",
    )(log_gamma, p_main, p_main, p_main, p_main, cos, sin, gn_gain.reshape(1, BRANCH_W))


def _gdn_gate_kernel(ab_ref, alog_ref, dt_ref, gb_ref, gbt_ref):
    c = SEQ_CHUNK
    x = ab_ref[0]
    sp = jnp.maximum(x + dt_ref[...], 0.0) + jnp.log1p(jnp.exp(-jnp.abs(x + dt_ref[...])))
    g = -jnp.exp(alog_ref[...]) * sp
    ri = lax.broadcasted_iota(jnp.int32, (c, c), 0)
    ci = lax.broadcasted_iota(jnp.int32, (c, c), 1)
    tri = jnp.where(ri >= ci, 1.0, 0.0).astype(F32)
    gc = jnp.dot(tri, g, preferred_element_type=F32, precision=lax.Precision.HIGHEST)
    lane = lax.broadcasted_iota(jnp.int32, x.shape, 1)
    out = jnp.where(lane < GDN_HEADS, gc, jax.nn.sigmoid(x))
    gb_ref[0] = out
    gbt_ref[0] = out.T


def _gdn_gates(p_ab, a_log, dt_bias):
    b, s, _ = p_ab.shape
    c = SEQ_CHUNK
    pad = LANE - GDN_HEADS
    alog = jnp.pad(a_log, (0, pad)).reshape(1, LANE)
    dt = jnp.pad(dt_bias, (0, pad)).reshape(1, LANE)
    return pl.pallas_call(
        _gdn_gate_kernel,
        grid=(b, s // c),
        in_specs=[pl.BlockSpec((1, c, LANE), lambda bi, ci: (bi, ci, 0)),
                  pl.BlockSpec((1, LANE), lambda bi, ci: (0, 0)),
                  pl.BlockSpec((1, LANE), lambda bi, ci: (0, 0))],
        out_specs=[pl.BlockSpec((1, c, LANE), lambda bi, ci: (bi, ci, 0)),
                   pl.BlockSpec((1, LANE, c), lambda bi, ci: (bi, 0, ci))],
        out_shape=[jax.ShapeDtypeStruct((b, s, LANE), F32),
                   jax.ShapeDtypeStruct((b, LANE, s), F32)],
        compiler_params=_cparams(("parallel", "parallel")),
        name="gdn_gates",
    )(p_ab, alog, dt)


def _block_levels(c):
    idx = np.arange(c)
    x = idx[:, None] ^ idx[None, :]
    return np.where(x > 0, np.floor(np.log2(np.maximum(x, 1))) + 1, 0).astype(np.float32)


def _unit_lower_inverse(a, lvl, eye):
    n = range(len(a))
    ab = [x.astype(BF16) for x in a]
    zero = jnp.zeros_like(ab[0])
    base_shift = 3
    a8 = [jnp.where(lvl <= base_shift, ab[i], zero) for i in n]
    t = [eye - a8[i].astype(F32) for i in n]
    a2 = [_dot(a8[i], a8[i]) for i in n]
    t = [t[i] + _mm(t[i], a2[i]) for i in n]
    a4 = [_mm(a2[i], a2[i]) for i in n]
    tb = [(t[i] + _mm(t[i], a4[i])).astype(BF16) for i in n]
    shift = base_shift
    while (1 << shift) < a[0].shape[0]:
        off = [jnp.where(lvl == shift + 1, ab[i], zero) for i in n]
        x = [_dot(tb[i], off[i]).astype(BF16) for i in n]
        tb = [tb[i] - _dot(x[i], tb[i]).astype(BF16) for i in n]
        shift += 1
    return tb


def _gdn_kernel(q_ref, k_ref, v_ref, z_ref, qh_ref, kh_ref, vh_ref, wq_ref, wk_ref, wv_ref,
                gb_ref, gbt_ref, lvl_ref, gain_ref, o_ref, state_ref):
    c = SEQ_CHUNK
    first = pl.program_id(2) == 0

    @pl.when(first)
    def _():
        state_ref[...] = jnp.zeros_like(state_ref)

    gb = gb_ref[0]
    lane = lax.broadcasted_iota(jnp.int32, gb.shape, 1)
    ri = lax.broadcasted_iota(jnp.int32, (c, c), 0)
    ci = lax.broadcasted_iota(jnp.int32, (c, c), 1)
    eye = jnp.where(ri == ci, 1.0, 0.0).astype(F32)
    lvl = lvl_ref[...]

    def conv_silu(x_ref, halo_ref, w_ref, sl):
        halo = jnp.where(first, 0.0, halo_ref[0, :, sl])
        xx = jnp.concatenate([halo, x_ref[0, :, sl]], axis=0)
        w = w_ref[:, sl]
        y = xx[SUBLANE:SUBLANE + c] * w[CONV_W - 1:CONV_W]
        for i in range(CONV_W - 1):
            off = SUBLANE - (CONV_W - 1) + i
            y = y + xx[off:off + c] * w[i:i + 1]
        return _silu(y)

    heads = range(GDN_HB)
    sls = [slice(j * GDN_DK, (j + 1) * GDN_DK) for j in heads]
    hs = [pl.program_id(1) * GDN_HB + j for j in heads]
    q = [conv_silu(q_ref, qh_ref, wq_ref, sl) for sl in sls]
    k = [conv_silu(k_ref, kh_ref, wk_ref, sl) for sl in sls]
    v = [conv_silu(v_ref, vh_ref, wv_ref, sl) for sl in sls]
    q = [x * lax.rsqrt(jnp.sum(x * x, axis=-1, keepdims=True) + EPS) * (GDN_DK ** -0.5) for x in q]
    k = [x * lax.rsqrt(jnp.sum(x * x, axis=-1, keepdims=True) + EPS) for x in k]

    gc_col = [jnp.sum(jnp.where(lane == h, gb, 0.0), axis=1, keepdims=True) for h in hs]
    beta = [jnp.sum(jnp.where(lane == h + GDN_HEADS, gb, 0.0), axis=1, keepdims=True) for h in hs]
    gc_row = [gbt_ref[0, pl.ds(h, 1), :] for h in hs]
    g_last = [x[c - 1:c, :] for x in gc_col]
    decay = [jnp.where(ri >= ci, jnp.exp(jnp.minimum(gc_col[j] - gc_row[j], 0.0)), 0.0) for j in heads]

    kb = [x.astype(BF16) for x in k]
    k_beta = [k[j] * beta[j] for j in heads]
    kk = [_mm_nt(k_beta[j], kb[j]) for j in heads]
    a_mat = [jnp.where(ri > ci, kk[j] * decay[j], 0.0) for j in heads]
    t_inv = _unit_lower_inverse(a_mat, lvl, eye)
    e_gc = [jnp.exp(x) for x in gc_col]
    rhs = [jnp.concatenate([v[j] * beta[j], k_beta[j] * e_gc[j]], axis=-1).astype(BF16) for j in heads]
    sol = [_dot(t_inv[j], rhs[j]) for j in heads]
    attn = [_mm_nt(q[j], kb[j]) * decay[j] for j in heads]
    q_g = [q[j] * e_gc[j] for j in heads]
    k_g = [k[j] * jnp.exp(g_last[j] - gc_col[j]) for j in heads]

    state = [state_ref[j] for j in heads]
    both = [_mm(jnp.concatenate([sol[j][:, GDN_DV:], q_g[j]], axis=0), state[j]) for j in heads]
    v_new = [(sol[j][:, :GDN_DV] - both[j][:c]).astype(BF16) for j in heads]
    o = [both[j][c:] + _mm(attn[j], v_new[j]) for j in heads]
    for j in heads:
        state_ref[j] = state[j] * jnp.exp(g_last[j]) + _mm(k_g[j].T, v_new[j])
    for j in heads:
        y = o[j] * lax.rsqrt(jnp.mean(o[j] * o[j], axis=-1, keepdims=True) + EPS) * gain_ref[...]
        o_ref[0, :, sls[j]] = (y * _silu(z_ref[0, :, sls[j]])).astype(o_ref.dtype)


def _gdn(p_main, conv_w, gb, gbt, norm_gain):
    b, s, _ = p_main.shape
    c = SEQ_CHUNK
    w = GDN_HB * GDN_DK
    groups = GDN_HEADS // GDN_HB
    qb = COL_GDN // w
    kb = qb + groups
    vb = kb + groups
    zb = vb + groups
    rows_per_chunk = c // SUBLANE

    def col(base):
        return pl.BlockSpec((1, c, w), lambda bi, hi, ci: (bi, ci, base + hi))

    def halo(base):
        return pl.BlockSpec((1, SUBLANE, w),
                            lambda bi, hi, ci: (bi, jnp.maximum(ci * rows_per_chunk - 1, 0), base + hi))

    def convw(base):
        return pl.BlockSpec((CONV_W, w), lambda bi, hi, ci: (0, base + hi))

    lvl = jnp.asarray(_block_levels(c), BF16)
    return pl.pallas_call(
        _gdn_kernel,
        grid=(b, groups, s // c),
        in_specs=[col(qb), col(kb), col(vb), col(zb),
                  halo(qb), halo(kb), halo(vb),
                  convw(0), convw(groups), convw(2 * groups),
                  pl.BlockSpec((1, c, LANE), lambda bi, hi, ci: (bi, ci, 0)),
                  pl.BlockSpec((1, LANE, c), lambda bi, hi, ci: (bi, 0, ci)),
                  pl.BlockSpec((c, c), lambda bi, hi, ci: (0, 0)),
                  pl.BlockSpec((1, GDN_DV), lambda bi, hi, ci: (0, 0))],
        out_specs=pl.BlockSpec((1, c, w), lambda bi, hi, ci: (bi, ci, hi)),
        out_shape=jax.ShapeDtypeStruct((b, s, BRANCH_W), BF16),
        scratch_shapes=[pltpu.VMEM((GDN_HB, GDN_DK, GDN_DV), F32)],
        compiler_params=_cparams(("parallel", "parallel", "arbitrary")),
        name="gdn",
    )(p_main, p_main, p_main, p_main, p_main, p_main, p_main, conv_w, conv_w, conv_w,
      gb, gbt, lvl, norm_gain.reshape(1, GDN_DV))


def _qknorm_kernel(q_ref, k_ref, v_ref, qg_ref, kg_ref, qo_ref, ko_ref, vo_ref):
    def norm(x_ref, g_ref, o_ref, scale):
        g = g_ref[...]
        for i in range(BRANCH_W // DIFF_DK):
            x = x_ref[0, :, i * DIFF_DK:(i + 1) * DIFF_DK]
            y = x * lax.rsqrt(jnp.mean(x * x, axis=-1, keepdims=True) + EPS) * g
            o_ref[0, :, i * DIFF_DK:(i + 1) * DIFF_DK] = (y * scale).astype(o_ref.dtype)

    norm(q_ref, qg_ref, qo_ref, DIFF_DK ** -0.5 * LOG2E)
    norm(k_ref, kg_ref, ko_ref, 1.0)
    vo_ref[...] = v_ref[...].astype(vo_ref.dtype)


def _qknorm(p_main, q_gain, k_gain):
    b, s, _ = p_main.shape
    c = SEQ_CHUNK
    base = COL_DIFF // BRANCH_W

    def col(j):
        return pl.BlockSpec((1, c, BRANCH_W), lambda bi, ci: (bi, ci, base + j))

    out = pl.BlockSpec((1, c, BRANCH_W), lambda bi, ci: (bi, ci, 0))
    gain = pl.BlockSpec((1, DIFF_DK), lambda bi, ci: (0, 0))
    shape = jax.ShapeDtypeStruct((b, s, BRANCH_W), BF16)
    return pl.pallas_call(
        _qknorm_kernel,
        grid=(b, s // c),
        in_specs=[col(0), col(1), col(2), gain, gain],
        out_specs=[out, out, out],
        out_shape=[shape, shape, shape],
        compiler_params=_cparams(("parallel", "parallel")),
        name="qknorm",
    )(p_main, p_main, p_main, q_gain.reshape(1, DIFF_DK), k_gain.reshape(1, DIFF_DK))


def _rel_bucket_np(rel):
    nb = NUM_BUCKETS // 2
    max_exact = nb // 2
    n = np.abs(rel)
    nf = np.maximum(n, 1).astype(np.float64)
    large = max_exact + (np.log(nf / max_exact) / math.log(MAX_DISTANCE / max_exact)
                         * (nb - max_exact)).astype(np.int32)
    large = np.minimum(large, nb - 1)
    return np.where(rel > 0, nb, 0) + np.where(n < max_exact, n, large)


FAR_BUCKET = NUM_BUCKETS // 2 - 1


def _bias_patterns():
    qpos = np.arange(ATT_TQ)[:, None]
    pats = []
    for j in range(ATT_NEAR):
        kpos = np.arange(ATT_TK)[None, :] + (j - 1) * ATT_TK
        bucket = _rel_bucket_np(kpos - qpos)
        visible = (kpos // MASK_CHUNK) <= (qpos // MASK_CHUNK)
        pats.append(np.where(visible, bucket, -1))
    return np.stack(pats).astype(np.int32)


def _bias_tile_kernel(rb_ref, pat_ref, o_ref):
    h = pl.program_id(0)
    pat = pat_ref[0]
    far = rb_ref[FAR_BUCKET, h]
    acc = jnp.full(pat.shape, NEG_MASK, F32)
    for bkt in range(NUM_BUCKETS):
        acc = jnp.where(pat == bkt, (rb_ref[bkt, h] - far) * LOG2E, acc)
    o_ref[0, 0] = acc


def _bias_tiles(rel_bias):
    pats = _bias_patterns()
    return pl.pallas_call(
        _bias_tile_kernel,
        grid=(DIFF_HEADS, ATT_NEAR),
        in_specs=[pl.BlockSpec(memory_space=pltpu.SMEM),
                  pl.BlockSpec((1, ATT_TQ, ATT_TK), lambda hi, pi: (pi, 0, 0))],
        out_specs=pl.BlockSpec((1, 1, ATT_TQ, ATT_TK), lambda hi, pi: (hi, pi, 0, 0)),
        out_shape=jax.ShapeDtypeStruct((DIFF_HEADS, ATT_NEAR, ATT_TQ, ATT_TK), F32),
        compiler_params=_cparams(("parallel", "parallel")),
        name="bias_tiles",
    )(rel_bias, jnp.asarray(pats))


def _diff_attn_kernel(q_ref, k_ref, v_ref, bias_ref, z_ref, lq1_ref, lk1_ref, lq2_ref, lk2_ref,
                      gain_ref, o_ref, s_ref, acc_ref, *, lam_init):
    tq, tk, u = ATT_TQ, ATT_TK, ATT_UNROLL
    r = tq // tk
    qi = pl.program_id(2)
    q = q_ref[0]
    qs = (q[:, :DIFF_DK], q[:, DIFF_DK:])

    def key_rows(c0, n):
        return pl.ds(pl.multiple_of(c0 * tk, tk), n * tk)

    def logits(c0, n):
        kt = k_ref[0, key_rows(c0, n), :]
        return [_mm_nt(qs[m], kt[:, m * DIFF_DK:(m + 1) * DIFF_DK]) for m in range(2)]

    def fold_max(mx, s):
        for j in range(s.shape[1] // LANE):
            mx = jnp.maximum(mx, s[:, j * LANE:(j + 1) * LANE])
        return mx

    def far_body(n, base):
        def body(i, mx):
            c0 = base + i * n
            out = []
            for m, s in enumerate(logits(c0, n)):
                for j in range(n):
                    s_ref[c0 + j, m] = s[:, j * tk:(j + 1) * tk]
                out.append(fold_max(mx[m], s))
            return tuple(out)
        return body

    def near(c0, j0, mx):
        out = []
        for m, s in enumerate(logits(c0, ATT_NEAR - j0)):
            acc = mx[m]
            for j in range(ATT_NEAR - j0):
                sj = s[:, j * tk:(j + 1) * tk] + bias_ref[0, j0 + j]
                s_ref[c0 + j, m] = sj
                acc = fold_max(acc, sj)
            out.append(acc)
        return tuple(out)

    n_far = jnp.maximum(r * qi - 1, 0)
    trips = n_far // u
    mx = (jnp.full((tq, LANE), NEG_MASK, F32),) * 2
    mx = lax.fori_loop(0, trips, far_body(u, 0), mx)
    mx = lax.fori_loop(0, n_far - trips * u, far_body(1, trips * u), mx)
    mx = lax.cond(qi >= 1, lambda v: near(n_far, 0, v), lambda v: near(0, 1, v), mx)
    mrep = [jnp.broadcast_to(jnp.max(v, axis=-1, keepdims=True), (tq, LANE)) for v in mx]

    acc_ref[...] = jnp.zeros_like(acc_ref)

    def pv_body(n, base):
        def body(i, lsum):
            c0 = base + i * n
            vt = v_ref[0, key_rows(c0, n), :]
            out = []
            for m in range(2):
                ps = []
                acc = lsum[m]
                for j in range(n):
                    s = s_ref[c0 + j, m]
                    for jj in range(tk // LANE):
                        p = jnp.exp2(s[:, jj * LANE:(jj + 1) * LANE] - mrep[m])
                        acc = acc + p
                        ps.append(p.astype(BF16))
                out.append(acc)
                acc_ref[m] += jnp.dot(jnp.concatenate(ps, axis=-1), vt, preferred_element_type=F32)
            return tuple(out)
        return body

    n_vis = r * (qi + 1)
    trips = n_vis // u
    lsum = (jnp.zeros((tq, LANE), F32),) * 2
    lsum = lax.fori_loop(0, trips, pv_body(u, 0), lsum)
    lsum = lax.fori_loop(0, n_vis - trips * u, pv_body(1, trips * u), lsum)
    l1, l2 = [jnp.sum(v, axis=-1, keepdims=True) for v in lsum]

    lam = (jnp.exp(jnp.sum(lq1_ref[...] * lk1_ref[...], axis=-1, keepdims=True))
           - jnp.exp(jnp.sum(lq2_ref[...] * lk2_ref[...], axis=-1, keepdims=True)) + lam_init)
    o = acc_ref[0] / l1 - lam * (acc_ref[1] / l2)
    o = o * lax.rsqrt(jnp.mean(o * o, axis=-1, keepdims=True) + EPS) * gain_ref[...]
    o = o * (1.0 - lam_init)
    o_ref[0] = (o * _silu(z_ref[0])).astype(o_ref.dtype)


def _diff_attention(qn, kn, vb, p_main, bias_tiles, lq1, lk1, lq2, lk2, subln_gain, lam_init):
    b, s, _ = qn.shape
    tq, tk = ATT_TQ, ATT_TK
    hw = 2 * DIFF_DK
    zb = (COL_DIFF + 3 * BRANCH_W) // DIFF_DV
    vec = pl.BlockSpec((1, DIFF_DK), lambda bi, hi, qi: (0, 0))
    r = lambda x: x.reshape(1, -1)
    return pl.pallas_call(
        functools.partial(_diff_attn_kernel, lam_init=lam_init),
        grid=(b, DIFF_HEADS, s // tq),
        in_specs=[pl.BlockSpec((1, tq, hw), lambda bi, hi, qi: (bi, qi, hi)),
                  pl.BlockSpec((1, s, hw), lambda bi, hi, qi: (bi, 0, hi)),
                  pl.BlockSpec((1, s, DIFF_DV), lambda bi, hi, qi: (bi, 0, hi)),
                  pl.BlockSpec((1, ATT_NEAR, tq, tk), lambda bi, hi, qi: (hi, 0, 0, 0)),
                  pl.BlockSpec((1, tq, DIFF_DV), lambda bi, hi, qi: (bi, qi, zb + hi)),
                  vec, vec, vec, vec,
                  pl.BlockSpec((1, DIFF_DV), lambda bi, hi, qi: (0, 0))],
        out_specs=pl.BlockSpec((1, tq, DIFF_DV), lambda bi, hi, qi: (bi, qi, hi)),
        out_shape=jax.ShapeDtypeStruct((b, s, BRANCH_W), BF16),
        scratch_shapes=[pltpu.VMEM((s // tk, 2, tq, tk), F32), pltpu.VMEM((2, tq, DIFF_DV), F32)],
        compiler_params=_cparams(("parallel", "parallel", "arbitrary")),
        name="diff_attention",
    )(qn, kn, vb, bias_tiles, p_main, r(lq1), r(lk1), r(lq2), r(lk2), r(subln_gain))


def _merge_kernel(y0_ref, y1_ref, y2_ref, w0_ref, w1_ref, w2_ref, g0_ref, g1_ref, g2_ref, o_ref):
    acc = None
    for y_ref, w_ref, g_ref in ((y0_ref, w0_ref, g0_ref), (y1_ref, w1_ref, g1_ref), (y2_ref, w2_ref, g2_ref)):
        term = jax.nn.sigmoid(g_ref[...]) * _dot(y_ref[...], w_ref[...])
        acc = term if acc is None else acc + term
    o_ref[...] = acc.astype(o_ref.dtype)


def _merge(ys, w_branch, layer, p_main2d):
    t = ys[0].shape[0]
    tm, tn = 256, 1024
    gate_base = COL_GATE // tn

    def wspec(n):
        return pl.BlockSpec((None, None, BRANCH_W, tn), lambda j, i: (layer, n, 0, j))

    def gspec(n):
        return pl.BlockSpec((tm, tn), lambda j, i: (i, gate_base + n * (D_MODEL // tn) + j))

    yspec = pl.BlockSpec((tm, BRANCH_W), lambda j, i: (i, 0))
    return pl.pallas_call(
        _merge_kernel,
        grid=(D_MODEL // tn, t // tm),
        in_specs=[yspec, yspec, yspec, wspec(0), wspec(1), wspec(2), gspec(0), gspec(1), gspec(2)],
        out_specs=pl.BlockSpec((tm, tn), lambda j, i: (i, j)),
        out_shape=jax.ShapeDtypeStruct((t, D_MODEL), BF16),
        compiler_params=_cparams(("parallel", "arbitrary")),
        name="merge",
    )(*ys, w_branch, w_branch, w_branch, p_main2d, p_main2d, p_main2d)


def _out_proj_kernel(a_ref, w_ref, x_ref, o_ref):
    o_ref[...] = x_ref[...] + _dot(a_ref[...], w_ref[...])


def _out_proj(merged, w_out, layer, x2d):
    t = merged.shape[0]
    tm, tn = 512, 1024
    return pl.pallas_call(
        _out_proj_kernel,
        grid=(D_MODEL // tn, t // tm),
        in_specs=[pl.BlockSpec((tm, D_MODEL), lambda j, i: (i, 0)),
                  pl.BlockSpec((None, D_MODEL, tn), lambda j, i: (layer, 0, j)),
                  pl.BlockSpec((tm, tn), lambda j, i: (i, j))],
        out_specs=pl.BlockSpec((tm, tn), lambda j, i: (i, j)),
        out_shape=jax.ShapeDtypeStruct((t, D_MODEL), F32),
        compiler_params=_cparams(("parallel", "arbitrary")),
        name="out_proj",
    )(merged, w_out, x2d)


def _rotary_tables(s):
    half = RET_DK // 2
    inv = ROPE_BASE ** (-np.arange(half, dtype=np.float64) / half)
    ang = np.arange(s, dtype=np.float64)[:, None] * inv[None, :]
    return jnp.asarray(np.cos(ang), F32), jnp.asarray(np.sin(ang), F32)


def _layer(x, layer, w_in, w2, w_branch, w_out, norm_gain, ret_gn_gain, gdn_conv_w, gdn_a_log,
           gdn_dt_bias, gdn_norm_gain, diff_q_gain, diff_k_gain, lq1, lk1, lq2, lk2, diff_subln_gain,
           bias_tiles, log_gamma, cos, sin):
    b, s, d = x.shape
    t = b * s
    x2d = x.reshape(t, d)

    h = _rmsnorm(x2d, norm_gain)
    p1_2d = _matmul(h, w_in, layer, 0, P1_COLS, F32, 512, 1024, "in_proj_1")
    p2_2d = _matmul(h, w2, layer, 0, P2_COLS, F32, 512, 1024, "in_proj_2")
    p_ab = _matmul(h, w_in, layer, AB_START // LANE, LANE, F32, 512, LANE, "in_proj_ab")
    p1 = p1_2d.reshape(b, s, P1_COLS)
    p2 = p2_2d.reshape(b, s, P2_COLS)

    y_ret = _retention(p1, ret_gn_gain, log_gamma, cos, sin)
    gb, gbt = _gdn_gates(p_ab.reshape(b, s, LANE), gdn_a_log, gdn_dt_bias)
    y_gdn = _gdn(p1, gdn_conv_w, gb, gbt, gdn_norm_gain)
    qn, kn, vb = _qknorm(p2, diff_q_gain, diff_k_gain)
    lam_init = 0.8 - 0.6 * math.exp(-0.3 * layer)
    y_diff = _diff_attention(qn, kn, vb, p2, bias_tiles, lq1, lk1, lq2, lk2,
                             diff_subln_gain, lam_init)

    ys = [y.reshape(t, BRANCH_W) for y in (y_ret, y_gdn, y_diff)]
    merged = _merge(ys, w_branch, layer, p2_2d)
    return _out_proj(merged, w_out, layer, x2d).reshape(b, s, d)


def kernel(x, norm_gain, w_in, ret_gn_gain, gdn_conv_w, gdn_a_log, gdn_dt_bias, gdn_norm_gain,
           diff_q_gain, diff_k_gain, diff_lambda_q1, diff_lambda_k1, diff_lambda_q2, diff_lambda_k2,
           diff_subln_gain, rel_bias, w_branch, w_out):
    depth = w_in.shape[0]
    s = x.shape[1]
    log_gamma = jnp.asarray(np.log(1.0 - 2.0 ** (-5.0 - np.arange(RET_HEADS, dtype=np.float64))), F32)
    cos, sin = _rotary_tables(s)
    bias_tiles = _bias_tiles(rel_bias)
    w_in_b = w_in.astype(BF16)
    w2 = w_in_b[:, :, AB_START + AB_COLS:]
    w_branch_b = w_branch.astype(BF16)
    w_out_b = w_out.astype(BF16)
    for l in range(depth):
        x = _layer(x, l, w_in_b, w2, w_branch_b, w_out_b, norm_gain[l], ret_gn_gain[l], gdn_conv_w[l],
                   gdn_a_log[l], gdn_dt_bias[l], gdn_norm_gain[l], diff_q_gain[l], diff_k_gain[l],
                   diff_lambda_q1[l], diff_lambda_k1[l], diff_lambda_q2[l], diff_lambda_k2[l],
                   diff_subln_gain[l], bias_tiles, log_gamma, cos, sin)
    return x
```

```python
import functools
import math

import numpy as np
import jax
import jax.numpy as jnp
from jax import lax
from jax.experimental import pallas as pl
from jax.experimental.pallas import tpu as pltpu

F32 = jnp.float32
BF16 = jnp.bfloat16

D_MODEL = 4096
EPS = 1e-6
BRANCH_W = 2048
N_BRANCH = 3

RET_HEADS = 8
RET_DK = 256
RET_DV = BRANCH_W // RET_HEADS
ROPE_BASE = 10000.0

GDN_HEADS = 16
GDN_DK = 128
GDN_DV = BRANCH_W // GDN_HEADS
CONV_W = 4

DIFF_HEADS = 8
DIFF_DK = 128
DIFF_DV = BRANCH_W // DIFF_HEADS
MASK_CHUNK = 64
NUM_BUCKETS = 32
MAX_DISTANCE = 128

AB_START = 16384
AB_COLS = 2 * GDN_HEADS
P1_COLS = 16384
P2_COLS = 20480
COL_RET = 0
COL_GDN = 8192
COL_DIFF = 0
COL_GATE = 8192

LANE = 128
SUBLANE = 8
SEQ_CHUNK = 256
NEG_MASK = -1e30
LOG2E = math.log2(math.e)
RET_HB = 2
GDN_HB = 4
ATT_TQ = 512
ATT_TK = 256
ATT_UNROLLS = (4, 2, 1)
ATT_NEAR = ATT_TQ // ATT_TK + 1
VMEM_LIMIT = 56 * 1024 * 1024


def _cparams(sem):
    return pltpu.CompilerParams(dimension_semantics=sem, vmem_limit_bytes=VMEM_LIMIT)


def _dot(a, b):
    return jnp.dot(a, b, preferred_element_type=F32)


def _mm(a, b):
    return _dot(a.astype(BF16), b.astype(BF16))


def _mm_nt(a, b):
    return lax.dot_general(a.astype(BF16), b.astype(BF16), (((1,), (1,)), ((), ())),
                           preferred_element_type=F32)


def _silu(x):
    return x * jax.nn.sigmoid(x)


def _rmsnorm_kernel(x_ref, g_ref, o_ref):
    x = x_ref[...]
    ms = jnp.mean(x * x, axis=-1, keepdims=True)
    o_ref[...] = (x * lax.rsqrt(ms + EPS) * g_ref[...]).astype(o_ref.dtype)


def _rmsnorm(x2d, gain):
    t, d = x2d.shape
    tm = 256
    return pl.pallas_call(
        _rmsnorm_kernel,
        grid=(t // tm,),
        in_specs=[pl.BlockSpec((tm, d), lambda i: (i, 0)),
                  pl.BlockSpec((1, d), lambda i: (0, 0))],
        out_specs=pl.BlockSpec((tm, d), lambda i: (i, 0)),
        out_shape=jax.ShapeDtypeStruct((t, d), BF16),
        compiler_params=_cparams(("parallel",)),
        name="rmsnorm",
    )(x2d, gain.reshape(1, d))


def _matmul_kernel(a_ref, w_ref, o_ref):
    o_ref[...] = _dot(a_ref[...], w_ref[...].astype(BF16)).astype(o_ref.dtype)


def _matmul(a, w, layer, col_block, n, out_dtype, tm, tn, name):
    m, k = a.shape
    return pl.pallas_call(
        _matmul_kernel,
        grid=(n // tn, m // tm),
        in_specs=[pl.BlockSpec((tm, k), lambda j, i: (i, 0)),
                  pl.BlockSpec((None, k, tn), lambda j, i: (layer, 0, col_block + j))],
        out_specs=pl.BlockSpec((tm, tn), lambda j, i: (i, j)),
        out_shape=jax.ShapeDtypeStruct((m, n), out_dtype),
        compiler_params=_cparams(("parallel", "arbitrary")),
        name=name,
    )(a, w)


def _retention_kernel(lg_ref, q_ref, k_ref, v_ref, z_ref, cos_ref, sin_ref, gain_ref,
                      o_ref, state_ref):
    c = SEQ_CHUNK

    @pl.when(pl.program_id(2) == 0)
    def _():
        state_ref[...] = jnp.zeros_like(state_ref)

    cos = cos_ref[...]
    sin = sin_ref[...]
    half = RET_DK // 2

    def rot(x):
        x1 = x[:, :half]
        x2 = x[:, half:]
        return jnp.concatenate([x1 * cos - x2 * sin, x1 * sin + x2 * cos], axis=-1)

    heads = range(RET_HB)
    sls = [slice(j * RET_DK, (j + 1) * RET_DK) for j in heads]
    lg = [lg_ref[pl.program_id(1) * RET_HB + j] for j in heads]
    ri = lax.broadcasted_iota(jnp.int32, (c, c), 0)
    ci = lax.broadcasted_iota(jnp.int32, (c, c), 1)
    rel = (ri - ci).astype(F32)
    rel_pos = jnp.maximum(rel, 0.0)
    idx = lax.broadcasted_iota(jnp.int32, (c, 1), 0).astype(F32)

    q = [rot(q_ref[0, :, sl]).astype(BF16) for sl in sls]
    k = [rot(k_ref[0, :, sl]) * (RET_DK ** -0.5) for sl in sls]
    v = [v_ref[0, :, sl].astype(BF16) for sl in sls]
    dmat = [jnp.where(rel >= 0, jnp.exp(rel_pos * lg[j]), 0.0) for j in heads]
    q_decay = [jnp.exp((idx + 1.0) * lg[j]) for j in heads]
    k_decay = [jnp.exp((c - 1.0 - idx) * lg[j]) for j in heads]
    chunk_decay = [jnp.exp(jnp.full((1, 1), float(c), F32) * lg[j]) for j in heads]

    scores = [_mm_nt(q[j], k[j]) * dmat[j] for j in heads]
    state = [state_ref[j] for j in heads]
    inter = [_mm(q[j], state[j]) * q_decay[j] for j in heads]
    intra = [_mm(scores[j], v[j]) for j in heads]
    for j in heads:
        state_ref[j] = state[j] * chunk_decay[j] + _mm((k[j] * k_decay[j]).T, v[j])
    for j in heads:
        o = intra[j] + inter[j]
        mu = jnp.mean(o, axis=-1, keepdims=True)
        var = jnp.mean(jnp.square(o - mu), axis=-1, keepdims=True)
        o = (o - mu) * lax.rsqrt(var + EPS) * gain_ref[:, sls[j]]
        o_ref[0, :, sls[j]] = (o * _silu(z_ref[0, :, sls[j]])).astype(o_ref.dtype)


def _retention(p_main, gn_gain, log_gamma, cos, sin):
    b, s, _ = p_main.shape
    c = SEQ_CHUNK
    w = RET_HB * RET_DK
    groups = RET_HEADS // RET_HB
    qb = COL_RET // w
    kb = qb + groups
    vb = kb + groups
    zb = vb + groups

    def col(base):
        return pl.BlockSpec((1, c, w), lambda bi, hi, ci: (bi, ci, base + hi))

    return pl.pallas_call(
        _retention_kernel,
        grid=(b, groups, s // c),
        in_specs=[pl.BlockSpec(memory_space=pltpu.SMEM),
                  col(qb), col(kb), col(vb), col(zb),
                  pl.BlockSpec((c, RET_DK // 2), lambda bi, hi, ci: (ci, 0)),
                  pl.BlockSpec((c, RET_DK // 2), lambda bi, hi, ci: (ci, 0)),
                  pl.BlockSpec((1, w), lambda bi, hi, ci: (0, hi))],
        out_specs=pl.BlockSpec((1, c, w), lambda bi, hi, ci: (bi, ci, hi)),
        out_shape=jax.ShapeDtypeStruct((b, s, BRANCH_W), BF16),
        scratch_shapes=[pltpu.VMEM((RET_HB, RET_DK, RET_DV), F32)],
        compiler_params=_cparams(("parallel", "parallel", "arbitrary")),
        name="retention",
    )(log_gamma, p_main, p_main, p_main, p_main, cos, sin, gn_gain.reshape(1, BRANCH_W))


def _gdn_gate_kernel(ab_ref, alog_ref, dt_ref, gb_ref, gbt_ref):
    c = SEQ_CHUNK
    x = ab_ref[0]
    sp = jnp.maximum(x + dt_ref[...], 0.0) + jnp.log1p(jnp.exp(-jnp.abs(x + dt_ref[...])))
    g = -jnp.exp(alog_ref[...]) * sp
    ri = lax.broadcasted_iota(jnp.int32, (c, c), 0)
    ci = lax.broadcasted_iota(jnp.int32, (c, c), 1)
    tri = jnp.where(ri >= ci, 1.0, 0.0).astype(F32)
    gc = jnp.dot(tri, g, preferred_element_type=F32, precision=lax.Precision.HIGHEST)
    lane = lax.broadcasted_iota(jnp.int32, x.shape, 1)
    out = jnp.where(lane < GDN_HEADS, gc, jax.nn.sigmoid(x))
    gb_ref[0] = out
    gbt_ref[0] = out.T


def _gdn_gates(p_ab, a_log, dt_bias):
    b, s, _ = p_ab.shape
    c = SEQ_CHUNK
    pad = LANE - GDN_HEADS
    alog = jnp.pad(a_log, (0, pad)).reshape(1, LANE)
    dt = jnp.pad(dt_bias, (0, pad)).reshape(1, LANE)
    return pl.pallas_call(
        _gdn_gate_kernel,
        grid=(b, s // c),
        in_specs=[pl.BlockSpec((1, c, LANE), lambda bi, ci: (bi, ci, 0)),
                  pl.BlockSpec((1, LANE), lambda bi, ci: (0, 0)),
                  pl.BlockSpec((1, LANE), lambda bi, ci: (0, 0))],
        out_specs=[pl.BlockSpec((1, c, LANE), lambda bi, ci: (bi, ci, 0)),
                   pl.BlockSpec((1, LANE, c), lambda bi, ci: (bi, 0, ci))],
        out_shape=[jax.ShapeDtypeStruct((b, s, LANE), F32),
                   jax.ShapeDtypeStruct((b, LANE, s), F32)],
        compiler_params=_cparams(("parallel", "parallel")),
        name="gdn_gates",
    )(p_ab, alog, dt)


def _block_levels(c):
    idx = np.arange(c)
    x = idx[:, None] ^ idx[None, :]
    return np.where(x > 0, np.floor(np.log2(np.maximum(x, 1))) + 1, 0).astype(np.float32)


def _unit_lower_inverse(a, lvl, eye):
    n = range(len(a))
    ab = [x.astype(BF16) for x in a]
    zero = jnp.zeros_like(ab[0])
    base_shift = 3
    a8 = [jnp.where(lvl <= base_shift, ab[i], zero) for i in n]
    t = [eye - a8[i].astype(F32) for i in n]
    a2 = [_dot(a8[i], a8[i]) for i in n]
    t = [t[i] + _mm(t[i], a2[i]) for i in n]
    a4 = [_mm(a2[i], a2[i]) for i in n]
    tb = [(t[i] + _mm(t[i], a4[i])).astype(BF16) for i in n]
    shift = base_shift
    while (1 << shift) < a[0].shape[0]:
        off = [jnp.where(lvl == shift + 1, ab[i], zero) for i in n]
        x = [_dot(tb[i], off[i]).astype(BF16) for i in n]
        tb = [tb[i] - _dot(x[i], tb[i]).astype(BF16) for i in n]
        shift += 1
    return tb


def _gdn_kernel(q_ref, k_ref, v_ref, z_ref, qh_ref, kh_ref, vh_ref, wq_ref, wk_ref, wv_ref,
                gb_ref, gbt_ref, lvl_ref, gain_ref, o_ref, state_ref):
    c = SEQ_CHUNK
    first = pl.program_id(2) == 0

    @pl.when(first)
    def _():
        state_ref[...] = jnp.zeros_like(state_ref)

    gb = gb_ref[0]
    lane = lax.broadcasted_iota(jnp.int32, gb.shape, 1)
    ri = lax.broadcasted_iota(jnp.int32, (c, c), 0)
    ci = lax.broadcasted_iota(jnp.int32, (c, c), 1)
    eye = jnp.where(ri == ci, 1.0, 0.0).astype(F32)
    lvl = lvl_ref[...]

    def conv_silu(x_ref, halo_ref, w_ref, sl):
        halo = jnp.where(first, 0.0, halo_ref[0, :, sl])
        xx = jnp.concatenate([halo, x_ref[0, :, sl]], axis=0)
        w = w_ref[:, sl]
        y = xx[SUBLANE:SUBLANE + c] * w[CONV_W - 1:CONV_W]
        for i in range(CONV_W - 1):
            off = SUBLANE - (CONV_W - 1) + i
            y = y + xx[off:off + c] * w[i:i + 1]
        return _silu(y)

    heads = range(GDN_HB)
    sls = [slice(j * GDN_DK, (j + 1) * GDN_DK) for j in heads]
    hs = [pl.program_id(1) * GDN_HB + j for j in heads]
    q = [conv_silu(q_ref, qh_ref, wq_ref, sl) for sl in sls]
    k = [conv_silu(k_ref, kh_ref, wk_ref, sl) for sl in sls]
    v = [conv_silu(v_ref, vh_ref, wv_ref, sl) for sl in sls]
    q = [x * lax.rsqrt(jnp.sum(x * x, axis=-1, keepdims=True) + EPS) * (GDN_DK ** -0.5) for x in q]
    k = [x * lax.rsqrt(jnp.sum(x * x, axis=-1, keepdims=True) + EPS) for x in k]

    gc_col = [jnp.sum(jnp.where(lane == h, gb, 0.0), axis=1, keepdims=True) for h in hs]
    beta = [jnp.sum(jnp.where(lane == h + GDN_HEADS, gb, 0.0), axis=1, keepdims=True) for h in hs]
    gc_row = [gbt_ref[0, pl.ds(h, 1), :] for h in hs]
    g_last = [x[c - 1:c, :] for x in gc_col]
    decay = [jnp.where(ri >= ci, jnp.exp(jnp.minimum(gc_col[j] - gc_row[j], 0.0)), 0.0) for j in heads]

    kb = [x.astype(BF16) for x in k]
    k_beta = [k[j] * beta[j] for j in heads]
    kk = [_mm_nt(k_beta[j], kb[j]) for j in heads]
    a_mat = [jnp.where(ri > ci, kk[j] * decay[j], 0.0) for j in heads]
    t_inv = _unit_lower_inverse(a_mat, lvl, eye)
    e_gc = [jnp.exp(x) for x in gc_col]
    rhs = [jnp.concatenate([v[j] * beta[j], k_beta[j] * e_gc[j]], axis=-1).astype(BF16) for j in heads]
    sol = [_dot(t_inv[j], rhs[j]) for j in heads]
    attn = [_mm_nt(q[j], kb[j]) * decay[j] for j in heads]
    q_g = [q[j] * e_gc[j] for j in heads]
    k_g = [k[j] * jnp.exp(g_last[j] - gc_col[j]) for j in heads]

    state = [state_ref[j] for j in heads]
    both = [_mm(jnp.concatenate([sol[j][:, GDN_DV:], q_g[j]], axis=0), state[j]) for j in heads]
    v_new = [(sol[j][:, :GDN_DV] - both[j][:c]).astype(BF16) for j in heads]
    o = [both[j][c:] + _mm(attn[j], v_new[j]) for j in heads]
    for j in heads:
        state_ref[j] = state[j] * jnp.exp(g_last[j]) + _mm(k_g[j].T, v_new[j])
    for j in heads:
        y = o[j] * lax.rsqrt(jnp.mean(o[j] * o[j], axis=-1, keepdims=True) + EPS) * gain_ref[...]
        o_ref[0, :, sls[j]] = (y * _silu(z_ref[0, :, sls[j]])).astype(o_ref.dtype)


def _gdn(p_main, conv_w, gb, gbt, norm_gain):
    b, s, _ = p_main.shape
    c = SEQ_CHUNK
    w = GDN_HB * GDN_DK
    groups = GDN_HEADS // GDN_HB
    qb = COL_GDN // w
    kb = qb + groups
    vb = kb + groups
    zb = vb + groups
    rows_per_chunk = c // SUBLANE

    def col(base):
        return pl.BlockSpec((1, c, w), lambda bi, hi, ci: (bi, ci, base + hi))

    def halo(base):
        return pl.BlockSpec((1, SUBLANE, w),
                            lambda bi, hi, ci: (bi, jnp.maximum(ci * rows_per_chunk - 1, 0), base + hi))

    def convw(base):
        return pl.BlockSpec((CONV_W, w), lambda bi, hi, ci: (0, base + hi))

    lvl = jnp.asarray(_block_levels(c), BF16)
    return pl.pallas_call(
        _gdn_kernel,
        grid=(b, groups, s // c),
        in_specs=[col(qb), col(kb), col(vb), col(zb),
                  halo(qb), halo(kb), halo(vb),
                  convw(0), convw(groups), convw(2 * groups),
                  pl.BlockSpec((1, c, LANE), lambda bi, hi, ci: (bi, ci, 0)),
                  pl.BlockSpec((1, LANE, c), lambda bi, hi, ci: (bi, 0, ci)),
                  pl.BlockSpec((c, c), lambda bi, hi, ci: (0, 0)),
                  pl.BlockSpec((1, GDN_DV), lambda bi, hi, ci: (0, 0))],
        out_specs=pl.BlockSpec((1, c, w), lambda bi, hi, ci: (bi, ci, hi)),
        out_shape=jax.ShapeDtypeStruct((b, s, BRANCH_W), BF16),
        scratch_shapes=[pltpu.VMEM((GDN_HB, GDN_DK, GDN_DV), F32)],
        compiler_params=_cparams(("parallel", "parallel", "arbitrary")),
        name="gdn",
    )(p_main, p_main, p_main, p_main, p_main, p_main, p_main, conv_w, conv_w, conv_w,
      gb, gbt, lvl, norm_gain.reshape(1, GDN_DV))


def _qknorm_kernel(q_ref, k_ref, v_ref, qg_ref, kg_ref, qo_ref, ko_ref, vo_ref):
    def norm(x_ref, g_ref, o_ref, scale):
        g = g_ref[...]
        for i in range(BRANCH_W // DIFF_DK):
            x = x_ref[0, :, i * DIFF_DK:(i + 1) * DIFF_DK]
            y = x * lax.rsqrt(jnp.mean(x * x, axis=-1, keepdims=True) + EPS) * g
            o_ref[0, :, i * DIFF_DK:(i + 1) * DIFF_DK] = (y * scale).astype(o_ref.dtype)

    norm(q_ref, qg_ref, qo_ref, DIFF_DK ** -0.5 * LOG2E)
    norm(k_ref, kg_ref, ko_ref, 1.0)
    vo_ref[...] = v_ref[...].astype(vo_ref.dtype)


def _qknorm(p_main, q_gain, k_gain):
    b, s, _ = p_main.shape
    c = SEQ_CHUNK
    base = COL_DIFF // BRANCH_W

    def col(j):
        return pl.BlockSpec((1, c, BRANCH_W), lambda bi, ci: (bi, ci, base + j))

    out = pl.BlockSpec((1, c, BRANCH_W), lambda bi, ci: (bi, ci, 0))
    gain = pl.BlockSpec((1, DIFF_DK), lambda bi, ci: (0, 0))
    shape = jax.ShapeDtypeStruct((b, s, BRANCH_W), BF16)
    return pl.pallas_call(
        _qknorm_kernel,
        grid=(b, s // c),
        in_specs=[col(0), col(1), col(2), gain, gain],
        out_specs=[out, out, out],
        out_shape=[shape, shape, shape],
        compiler_params=_cparams(("parallel", "parallel")),
        name="qknorm",
    )(p_main, p_main, p_main, q_gain.reshape(1, DIFF_DK), k_gain.reshape(1, DIFF_DK))


def _rel_bucket_np(rel):
    nb = NUM_BUCKETS // 2
    max_exact = nb // 2
    n = np.abs(rel)
    nf = np.maximum(n, 1).astype(np.float64)
    large = max_exact + (np.log(nf / max_exact) / math.log(MAX_DISTANCE / max_exact)
                         * (nb - max_exact)).astype(np.int32)
    large = np.minimum(large, nb - 1)
    return np.where(rel > 0, nb, 0) + np.where(n < max_exact, n, large)


FAR_BUCKET = NUM_BUCKETS // 2 - 1


def _bias_patterns():
    qpos = np.arange(ATT_TQ)[:, None]
    pats = []
    for j in range(ATT_NEAR):
        kpos = np.arange(ATT_TK)[None, :] + (j - 1) * ATT_TK
        bucket = _rel_bucket_np(kpos - qpos)
        visible = (kpos // MASK_CHUNK) <= (qpos // MASK_CHUNK)
        pats.append(np.where(visible, bucket, -1))
    return np.stack(pats).astype(np.int32)


def _bias_tile_kernel(rb_ref, pat_ref, o_ref):
    h = pl.program_id(0)
    pat = pat_ref[0]
    far = rb_ref[FAR_BUCKET, h]
    acc = jnp.full(pat.shape, NEG_MASK, F32)
    for bkt in range(NUM_BUCKETS):
        acc = jnp.where(pat == bkt, (rb_ref[bkt, h] - far) * LOG2E, acc)
    o_ref[0, 0] = acc


def _bias_tiles(rel_bias):
    pats = _bias_patterns()
    return pl.pallas_call(
        _bias_tile_kernel,
        grid=(DIFF_HEADS, ATT_NEAR),
        in_specs=[pl.BlockSpec(memory_space=pltpu.SMEM),
                  pl.BlockSpec((1, ATT_TQ, ATT_TK), lambda hi, pi: (pi, 0, 0))],
        out_specs=pl.BlockSpec((1, 1, ATT_TQ, ATT_TK), lambda hi, pi: (hi, pi, 0, 0)),
        out_shape=jax.ShapeDtypeStruct((DIFF_HEADS, ATT_NEAR, ATT_TQ, ATT_TK), F32),
        compiler_params=_cparams(("parallel", "parallel")),
        name="bias_tiles",
    )(rel_bias, jnp.asarray(pats))


def _diff_attn_kernel(q_ref, k_ref, v_ref, bias_ref, z_ref, lq1_ref, lk1_ref, lq2_ref, lk2_ref,
                      gain_ref, o_ref, s_ref, acc_ref, *, lam_init):
    tq, tk = ATT_TQ, ATT_TK
    r = tq // tk
    qi = pl.program_id(2)
    q = q_ref[0]
    qs = (q[:, :DIFF_DK], q[:, DIFF_DK:])

    def key_rows(c0, n):
        return pl.ds(pl.multiple_of(c0 * tk, tk), n * tk)

    def logits(c0, n):
        kt = k_ref[0, key_rows(c0, n), :]
        return [_mm_nt(qs[m], kt[:, m * DIFF_DK:(m + 1) * DIFF_DK]) for m in range(2)]

    def fold_max(mx, s):
        for j in range(s.shape[1] // LANE):
            mx = jnp.maximum(mx, s[:, j * LANE:(j + 1) * LANE])
        return mx

    def far_body(n, base):
        def body(i, mx):
            c0 = base + i * n
            out = []
            for m, s in enumerate(logits(c0, n)):
                for j in range(n):
                    s_ref[c0 + j, m] = s[:, j * tk:(j + 1) * tk]
                out.append(fold_max(mx[m], s))
            return tuple(out)
        return body

    def near(c0, j0, mx):
        out = []
        for m, s in enumerate(logits(c0, ATT_NEAR - j0)):
            acc = mx[m]
            for j in range(ATT_NEAR - j0):
                sj = s[:, j * tk:(j + 1) * tk] + bias_ref[0, j0 + j]
                s_ref[c0 + j, m] = sj
                acc = fold_max(acc, sj)
            out.append(acc)
        return tuple(out)

    def sweep(n_tiles, make_body, carry):
        base = 0
        for width in ATT_UNROLLS:
            trips = (n_tiles - base) // width
            carry = lax.fori_loop(0, trips, make_body(width, base), carry)
            base = base + trips * width
        return carry

    n_far = jnp.maximum(r * qi - 1, 0)
    mx = sweep(n_far, far_body, (jnp.full((tq, LANE), NEG_MASK, F32),) * 2)
    mx = lax.cond(qi >= 1, lambda v: near(n_far, 0, v), lambda v: near(0, 1, v), mx)
    mrep = [jnp.broadcast_to(jnp.max(v, axis=-1, keepdims=True), (tq, LANE)) for v in mx]

    acc_ref[...] = jnp.zeros_like(acc_ref)

    def pv_body(n, base):
        def body(i, lsum):
            c0 = base + i * n
            vt = v_ref[0, key_rows(c0, n), :]
            out = []
            for m in range(2):
                ps = []
                acc = lsum[m]
                for j in range(n):
                    s = s_ref[c0 + j, m]
                    for jj in range(tk // LANE):
                        p = jnp.exp2(s[:, jj * LANE:(jj + 1) * LANE] - mrep[m])
                        acc = acc + p
                        ps.append(p.astype(BF16))
                out.append(acc)
                acc_ref[m] += jnp.dot(jnp.concatenate(ps, axis=-1), vt, preferred_element_type=F32)
            return tuple(out)
        return body

    lsum = sweep(r * (qi + 1), pv_body, (jnp.zeros((tq, LANE), F32),) * 2)
    l1, l2 = [jnp.sum(v, axis=-1, keepdims=True) for v in lsum]

    lam = (jnp.exp(jnp.sum(lq1_ref[...] * lk1_ref[...], axis=-1, keepdims=True))
           - jnp.exp(jnp.sum(lq2_ref[...] * lk2_ref[...], axis=-1, keepdims=True)) + lam_init)
    o = acc_ref[0] / l1 - lam * (acc_ref[1] / l2)
    o = o * lax.rsqrt(jnp.mean(o * o, axis=-1, keepdims=True) + EPS) * gain_ref[...]
    o = o * (1.0 - lam_init)
    o_ref[0] = (o * _silu(z_ref[0])).astype(o_ref.dtype)


def _diff_attention(qn, kn, vb, p_main, bias_tiles, lq1, lk1, lq2, lk2, subln_gain, lam_init):
    b, s, _ = qn.shape
    tq, tk = ATT_TQ, ATT_TK
    hw = 2 * DIFF_DK
    zb = (COL_DIFF + 3 * BRANCH_W) // DIFF_DV
    vec = pl.BlockSpec((1, DIFF_DK), lambda bi, hi, qi: (0, 0))
    r = lambda x: x.reshape(1, -1)
    return pl.pallas_call(
        functools.partial(_diff_attn_kernel, lam_init=lam_init),
        grid=(b, DIFF_HEADS, s // tq),
        in_specs=[pl.BlockSpec((1, tq, hw), lambda bi, hi, qi: (bi, qi, hi)),
                  pl.BlockSpec((1, s, hw), lambda bi, hi, qi: (bi, 0, hi)),
                  pl.BlockSpec((1, s, DIFF_DV), lambda bi, hi, qi: (bi, 0, hi)),
                  pl.BlockSpec((1, ATT_NEAR, tq, tk), lambda bi, hi, qi: (hi, 0, 0, 0)),
                  pl.BlockSpec((1, tq, DIFF_DV), lambda bi, hi, qi: (bi, qi, zb + hi)),
                  vec, vec, vec, vec,
                  pl.BlockSpec((1, DIFF_DV), lambda bi, hi, qi: (0, 0))],
        out_specs=pl.BlockSpec((1, tq, DIFF_DV), lambda bi, hi, qi: (bi, qi, hi)),
        out_shape=jax.ShapeDtypeStruct((b, s, BRANCH_W), BF16),
        scratch_shapes=[pltpu.VMEM((s // tk, 2, tq, tk), F32), pltpu.VMEM((2, tq, DIFF_DV), F32)],
        compiler_params=_cparams(("parallel", "parallel", "arbitrary")),
        name="diff_attention",
    )(qn, kn, vb, bias_tiles, p_main, r(lq1), r(lk1), r(lq2), r(lk2), r(subln_gain))


def _merge_kernel(y0_ref, y1_ref, y2_ref, w0_ref, w1_ref, w2_ref, g0_ref, g1_ref, g2_ref, o_ref):
    acc = None
    for y_ref, w_ref, g_ref in ((y0_ref, w0_ref, g0_ref), (y1_ref, w1_ref, g1_ref), (y2_ref, w2_ref, g2_ref)):
        term = jax.nn.sigmoid(g_ref[...]) * _dot(y_ref[...], w_ref[...])
        acc = term if acc is None else acc + term
    o_ref[...] = acc.astype(o_ref.dtype)


def _merge(ys, w_branch, layer, p_main2d):
    t = ys[0].shape[0]
    tm, tn = 256, 1024
    gate_base = COL_GATE // tn

    def wspec(n):
        return pl.BlockSpec((None, None, BRANCH_W, tn), lambda j, i: (layer, n, 0, j))

    def gspec(n):
        return pl.BlockSpec((tm, tn), lambda j, i: (i, gate_base + n * (D_MODEL // tn) + j))

    yspec = pl.BlockSpec((tm, BRANCH_W), lambda j, i: (i, 0))
    return pl.pallas_call(
        _merge_kernel,
        grid=(D_MODEL // tn, t // tm),
        in_specs=[yspec, yspec, yspec, wspec(0), wspec(1), wspec(2), gspec(0), gspec(1), gspec(2)],
        out_specs=pl.BlockSpec((tm, tn), lambda j, i: (i, j)),
        out_shape=jax.ShapeDtypeStruct((t, D_MODEL), BF16),
        compiler_params=_cparams(("parallel", "arbitrary")),
        name="merge",
    )(*ys, w_branch, w_branch, w_branch, p_main2d, p_main2d, p_main2d)


def _out_proj_kernel(a_ref, w_ref, x_ref, o_ref):
    o_ref[...] = x_ref[...] + _dot(a_ref[...], w_ref[...])


def _out_proj(merged, w_out, layer, x2d):
    t = merged.shape[0]
    tm, tn = 512, 1024
    return pl.pallas_call(
        _out_proj_kernel,
        grid=(D_MODEL // tn, t // tm),
        in_specs=[pl.BlockSpec((tm, D_MODEL), lambda j, i: (i, 0)),
                  pl.BlockSpec((None, D_MODEL, tn), lambda j, i: (layer, 0, j)),
                  pl.BlockSpec((tm, tn), lambda j, i: (i, j))],
        out_specs=pl.BlockSpec((tm, tn), lambda j, i: (i, j)),
        out_shape=jax.ShapeDtypeStruct((t, D_MODEL), F32),
        compiler_params=_cparams(("parallel", "arbitrary")),
        name="out_proj",
    )(merged, w_out, x2d)


def _rotary_tables(s):
    half = RET_DK // 2
    inv = ROPE_BASE ** (-np.arange(half, dtype=np.float64) / half)
    ang = np.arange(s, dtype=np.float64)[:, None] * inv[None, :]
    return jnp.asarray(np.cos(ang), F32), jnp.asarray(np.sin(ang), F32)


def _layer(x, layer, w_in, w2, w_branch, w_out, norm_gain, ret_gn_gain, gdn_conv_w, gdn_a_log,
           gdn_dt_bias, gdn_norm_gain, diff_q_gain, diff_k_gain, lq1, lk1, lq2, lk2, diff_subln_gain,
           bias_tiles, log_gamma, cos, sin):
    b, s, d = x.shape
    t = b * s
    x2d = x.reshape(t, d)

    h = _rmsnorm(x2d, norm_gain)
    p1_2d = _matmul(h, w_in, layer, 0, P1_COLS, F32, 512, 1024, "in_proj_1")
    p2_2d = _matmul(h, w2, layer, 0, P2_COLS, F32, 512, 1024, "in_proj_2")
    p_ab = _matmul(h, w_in, layer, AB_START // LANE, LANE, F32, 512, LANE, "in_proj_ab")
    p1 = p1_2d.reshape(b, s, P1_COLS)
    p2 = p2_2d.reshape(b, s, P2_COLS)

    y_ret = _retention(p1, ret_gn_gain, log_gamma, cos, sin)
    gb, gbt = _gdn_gates(p_ab.reshape(b, s, LANE), gdn_a_log, gdn_dt_bias)
    y_gdn = _gdn(p1, gdn_conv_w, gb, gbt, gdn_norm_gain)
    qn, kn, vb = _qknorm(p2, diff_q_gain, diff_k_gain)
    lam_init = 0.8 - 0.6 * math.exp(-0.3 * layer)
    y_diff = _diff_attention(qn, kn, vb, p2, bias_tiles, lq1, lk1, lq2, lk2,
                             diff_subln_gain, lam_init)

    ys = [y.reshape(t, BRANCH_W) for y in (y_ret, y_gdn, y_diff)]
    merged = _merge(ys, w_branch, layer, p2_2d)
    return _out_proj(merged, w_out, layer, x2d).reshape(b, s, d)


def kernel(x, norm_gain, w_in, ret_gn_gain, gdn_conv_w, gdn_a_log, gdn_dt_bias, gdn_norm_gain,
           diff_q_gain, diff_k_gain, diff_lambda_q1, diff_lambda_k1, diff_lambda_q2, diff_lambda_k2,
           diff_subln_gain, rel_bias, w_branch, w_out):
    depth = w_in.shape[0]
    s = x.shape[1]
    log_gamma = jnp.asarray(np.log(1.0 - 2.0 ** (-5.0 - np.arange(RET_HEADS, dtype=np.float64))), F32)
    cos, sin = _rotary_tables(s)
    bias_tiles = _bias_tiles(rel_bias)
    w_in_b = w_in.astype(BF16)
    w2 = w_in_b[:, :, AB_START + AB_COLS:]
    w_branch_b = w_branch.astype(BF16)
    w_out_b = w_out.astype(BF16)
    for l in range(depth):
        x = _layer(x, l, w_in_b, w2, w_branch_b, w_out_b, norm_gain[l], ret_gn_gain[l], gdn_conv_w[l],
                   gdn_a_log[l], gdn_dt_bias[l], gdn_norm_gain[l], diff_q_gain[l], diff_k_gain[l],
                   diff_lambda_q1[l], diff_lambda_k1[l], diff_lambda_q2[l], diff_lambda_k2[l],
                   diff_subln_gain[l], bias_tiles, log_gamma, cos, sin)
    return x
```

```python
import functools
import math

import numpy as np
import jax
import jax.numpy as jnp
from jax import lax
from jax.experimental import pallas as pl
from jax.experimental.pallas import tpu as pltpu

F32 = jnp.float32
BF16 = jnp.bfloat16

D_MODEL = 4096
EPS = 1e-6
BRANCH_W = 2048
N_BRANCH = 3

RET_HEADS = 8
RET_DK = 256
RET_DV = BRANCH_W // RET_HEADS
ROPE_BASE = 10000.0

GDN_HEADS = 16
GDN_DK = 128
GDN_DV = BRANCH_W // GDN_HEADS
CONV_W = 4

DIFF_HEADS = 8
DIFF_DK = 128
DIFF_DV = BRANCH_W // DIFF_HEADS
MASK_CHUNK = 64
NUM_BUCKETS = 32
MAX_DISTANCE = 128

AB_START = 16384
AB_COLS = 2 * GDN_HEADS
P1_COLS = 16384
P2_COLS = 20480
COL_RET = 0
COL_GDN = 8192
COL_DIFF = 0
COL_GATE = 8192

LANE = 128
SUBLANE = 8
SEQ_CHUNK = 256
NEG_MASK = -1e30
LOG2E = math.log2(math.e)
RET_HB = 2
GDN_HB = 4
ATT_TQ = 512
ATT_TK = 256
ATT_UNROLLS = (4, 2, 1)
ATT_NEAR = ATT_TQ // ATT_TK + 1
VMEM_LIMIT = 56 * 1024 * 1024


def _cparams(sem):
    return pltpu.CompilerParams(dimension_semantics=sem, vmem_limit_bytes=VMEM_LIMIT)


def _dot(a, b):
    return jnp.dot(a, b, preferred_element_type=F32)


def _mm(a, b):
    return _dot(a.astype(BF16), b.astype(BF16))


def _mm_nt(a, b):
    return lax.dot_general(a.astype(BF16), b.astype(BF16), (((1,), (1,)), ((), ())),
                           preferred_element_type=F32)


def _silu(x):
    return x * jax.nn.sigmoid(x)


def _rmsnorm_kernel(x_ref, g_ref, o_ref):
    x = x_ref[...]
    ms = jnp.mean(x * x, axis=-1, keepdims=True)
    o_ref[...] = (x * lax.rsqrt(ms + EPS) * g_ref[...]).astype(o_ref.dtype)


def _rmsnorm(x2d, gain):
    t, d = x2d.shape
    tm = 256
    return pl.pallas_call(
        _rmsnorm_kernel,
        grid=(t // tm,),
        in_specs=[pl.BlockSpec((tm, d), lambda i: (i, 0)),
                  pl.BlockSpec((1, d), lambda i: (0, 0))],
        out_specs=pl.BlockSpec((tm, d), lambda i: (i, 0)),
        out_shape=jax.ShapeDtypeStruct((t, d), BF16),
        compiler_params=_cparams(("parallel",)),
        name="rmsnorm",
    )(x2d, gain.reshape(1, d))


def _matmul_kernel(a_ref, w_ref, o_ref):
    o_ref[...] = _dot(a_ref[...], w_ref[...].astype(BF16)).astype(o_ref.dtype)


def _matmul(a, w, layer, col_block, n, out_dtype, tm, tn, name):
    m, k = a.shape
    return pl.pallas_call(
        _matmul_kernel,
        grid=(n // tn, m // tm),
        in_specs=[pl.BlockSpec((tm, k), lambda j, i: (i, 0)),
                  pl.BlockSpec((None, k, tn), lambda j, i: (layer, 0, col_block + j))],
        out_specs=pl.BlockSpec((tm, tn), lambda j, i: (i, j)),
        out_shape=jax.ShapeDtypeStruct((m, n), out_dtype),
        compiler_params=_cparams(("parallel", "arbitrary")),
        name=name,
    )(a, w)


def _retention_kernel(lg_ref, q_ref, k_ref, v_ref, z_ref, cos_ref, sin_ref, gain_ref,
                      o_ref, state_ref):
    c = SEQ_CHUNK

    @pl.when(pl.program_id(2) == 0)
    def _():
        state_ref[...] = jnp.zeros_like(state_ref)

    cos = cos_ref[...]
    sin = sin_ref[...]
    half = RET_DK // 2

    def rot(x):
        x1 = x[:, :half]
        x2 = x[:, half:]
        return jnp.concatenate([x1 * cos - x2 * sin, x1 * sin + x2 * cos], axis=-1)

    heads = range(RET_HB)
    sls = [slice(j * RET_DK, (j + 1) * RET_DK) for j in heads]
    lg = [lg_ref[pl.program_id(1) * RET_HB + j] for j in heads]
    ri = lax.broadcasted_iota(jnp.int32, (c, c), 0)
    ci = lax.broadcasted_iota(jnp.int32, (c, c), 1)
    rel = (ri - ci).astype(F32)
    rel_pos = jnp.maximum(rel, 0.0)
    idx = lax.broadcasted_iota(jnp.int32, (c, 1), 0).astype(F32)

    q = [rot(q_ref[0, :, sl]).astype(BF16) for sl in sls]
    k = [rot(k_ref[0, :, sl]) * (RET_DK ** -0.5) for sl in sls]
    v = [v_ref[0, :, sl].astype(BF16) for sl in sls]
    dmat = [jnp.where(rel >= 0, jnp.exp(rel_pos * lg[j]), 0.0) for j in heads]
    q_decay = [jnp.exp((idx + 1.0) * lg[j]) for j in heads]
    k_decay = [jnp.exp((c - 1.0 - idx) * lg[j]) for j in heads]
    chunk_decay = [jnp.exp(jnp.full((1, 1), float(c), F32) * lg[j]) for j in heads]

    scores = [_mm_nt(q[j], k[j]) * dmat[j] for j in heads]
    state = [state_ref[j] for j in heads]
    inter = [_mm(q[j], state[j]) * q_decay[j] for j in heads]
    intra = [_mm(scores[j], v[j]) for j in heads]
    for j in heads:
        state_ref[j] = state[j] * chunk_decay[j] + _mm((k[j] * k_decay[j]).T, v[j])
    for j in heads:
        o = intra[j] + inter[j]
        mu = jnp.mean(o, axis=-1, keepdims=True)
        var = jnp.mean(jnp.square(o - mu), axis=-1, keepdims=True)
        o = (o - mu) * lax.rsqrt(var + EPS) * gain_ref[:, sls[j]]
        o_ref[0, :, sls[j]] = (o * _silu(z_ref[0, :, sls[j]])).astype(o_ref.dtype)


def _retention(p_main, gn_gain, log_gamma, cos, sin):
    b, s, _ = p_main.shape
    c = SEQ_CHUNK
    w = RET_HB * RET_DK
    groups = RET_HEADS // RET_HB
    qb = COL_RET // w
    kb = qb + groups
    vb = kb + groups
    zb = vb + groups

    def col(base):
        return pl.BlockSpec((1, c, w), lambda bi, hi, ci: (bi, ci, base + hi))

    return pl.pallas_call(
        _retention_kernel,
        grid=(b, groups, s // c),
        in_specs=[pl.BlockSpec(memory_space=pltpu.SMEM),
                  col(qb), col(kb), col(vb), col(zb),
                  pl.BlockSpec((c, RET_DK // 2), lambda bi, hi, ci: (ci, 0)),
                  pl.BlockSpec((c, RET_DK // 2), lambda bi, hi, ci: (ci, 0)),
                  pl.BlockSpec((1, w), lambda bi, hi, ci: (0, hi))],
        out_specs=pl.BlockSpec((1, c, w), lambda bi, hi, ci: (bi, ci, hi)),
        out_shape=jax.ShapeDtypeStruct((b, s, BRANCH_W), BF16),
        scratch_shapes=[pltpu.VMEM((RET_HB, RET_DK, RET_DV), F32)],
        compiler_params=_cparams(("parallel", "parallel", "arbitrary")),
        name="retention",
    )(log_gamma, p_main, p_main, p_main, p_main, cos, sin, gn_gain.reshape(1, BRANCH_W))


def _gdn_gate_kernel(ab_ref, alog_ref, dt_ref, gb_ref, gbt_ref):
    c = SEQ_CHUNK
    x = ab_ref[0]
    sp = jnp.maximum(x + dt_ref[...], 0.0) + jnp.log1p(jnp.exp(-jnp.abs(x + dt_ref[...])))
    g = -jnp.exp(alog_ref[...]) * sp
    ri = lax.broadcasted_iota(jnp.int32, (c, c), 0)
    ci = lax.broadcasted_iota(jnp.int32, (c, c), 1)
    tri = jnp.where(ri >= ci, 1.0, 0.0).astype(F32)
    gc = jnp.dot(tri, g, preferred_element_type=F32, precision=lax.Precision.HIGHEST)
    lane = lax.broadcasted_iota(jnp.int32, x.shape, 1)
    out = jnp.where(lane < GDN_HEADS, gc, jax.nn.sigmoid(x))
    gb_ref[0] = out
    gbt_ref[0] = out.T


def _gdn_gates(p_ab, a_log, dt_bias):
    b, s, _ = p_ab.shape
    c = SEQ_CHUNK
    pad = LANE - GDN_HEADS
    alog = jnp.pad(a_log, (0, pad)).reshape(1, LANE)
    dt = jnp.pad(dt_bias, (0, pad)).reshape(1, LANE)
    return pl.pallas_call(
        _gdn_gate_kernel,
        grid=(b, s // c),
        in_specs=[pl.BlockSpec((1, c, LANE), lambda bi, ci: (bi, ci, 0)),
                  pl.BlockSpec((1, LANE), lambda bi, ci: (0, 0)),
                  pl.BlockSpec((1, LANE), lambda bi, ci: (0, 0))],
        out_specs=[pl.BlockSpec((1, c, LANE), lambda bi, ci: (bi, ci, 0)),
                   pl.BlockSpec((1, LANE, c), lambda bi, ci: (bi, 0, ci))],
        out_shape=[jax.ShapeDtypeStruct((b, s, LANE), F32),
                   jax.ShapeDtypeStruct((b, LANE, s), F32)],
        compiler_params=_cparams(("parallel", "parallel")),
        name="gdn_gates",
    )(p_ab, alog, dt)


def _block_levels(c):
    idx = np.arange(c)
    x = idx[:, None] ^ idx[None, :]
    return np.where(x > 0, np.floor(np.log2(np.maximum(x, 1))) + 1, 0).astype(np.float32)


def _unit_lower_inverse(a, lvl, eye):
    n = range(len(a))
    ab = [x.astype(BF16) for x in a]
    zero = jnp.zeros_like(ab[0])
    base_shift = 3
    a8 = [jnp.where(lvl <= base_shift, ab[i], zero) for i in n]
    t = [eye - a8[i].astype(F32) for i in n]
    a2 = [_dot(a8[i], a8[i]) for i in n]
    t = [t[i] + _mm(t[i], a2[i]) for i in n]
    a4 = [_mm(a2[i], a2[i]) for i in n]
    tb = [(t[i] + _mm(t[i], a4[i])).astype(BF16) for i in n]
    shift = base_shift
    while (1 << shift) < a[0].shape[0]:
        off = [jnp.where(lvl == shift + 1, ab[i], zero) for i in n]
        x = [_dot(tb[i], off[i]).astype(BF16) for i in n]
        tb = [tb[i] - _dot(x[i], tb[i]).astype(BF16) for i in n]
        shift += 1
    return tb


def _gdn_kernel(q_ref, k_ref, v_ref, z_ref, qh_ref, kh_ref, vh_ref, wq_ref, wk_ref, wv_ref,
                gb_ref, gbt_ref, lvl_ref, gain_ref, o_ref, state_ref):
    c = SEQ_CHUNK
    first = pl.program_id(2) == 0

    @pl.when(first)
    def _():
        state_ref[...] = jnp.zeros_like(state_ref)

    gb = gb_ref[0]
    lane = lax.broadcasted_iota(jnp.int32, gb.shape, 1)
    ri = lax.broadcasted_iota(jnp.int32, (c, c), 0)
    ci = lax.broadcasted_iota(jnp.int32, (c, c), 1)
    eye = jnp.where(ri == ci, 1.0, 0.0).astype(F32)
    lvl = lvl_ref[...]

    def conv_silu(x_ref, halo_ref, w_ref, sl):
        halo = jnp.where(first, 0.0, halo_ref[0, :, sl])
        xx = jnp.concatenate([halo, x_ref[0, :, sl]], axis=0)
        w = w_ref[:, sl]
        y = xx[SUBLANE:SUBLANE + c] * w[CONV_W - 1:CONV_W]
        for i in range(CONV_W - 1):
            off = SUBLANE - (CONV_W - 1) + i
            y = y + xx[off:off + c] * w[i:i + 1]
        return _silu(y)

    heads = range(GDN_HB)
    sls = [slice(j * GDN_DK, (j + 1) * GDN_DK) for j in heads]
    hs = [pl.program_id(1) * GDN_HB + j for j in heads]
    q = [conv_silu(q_ref, qh_ref, wq_ref, sl) for sl in sls]
    k = [conv_silu(k_ref, kh_ref, wk_ref, sl) for sl in sls]
    v = [conv_silu(v_ref, vh_ref, wv_ref, sl) for sl in sls]
    q = [x * lax.rsqrt(jnp.sum(x * x, axis=-1, keepdims=True) + EPS) * (GDN_DK ** -0.5) for x in q]
    k = [x * lax.rsqrt(jnp.sum(x * x, axis=-1, keepdims=True) + EPS) for x in k]

    gc_col = [jnp.sum(jnp.where(lane == h, gb, 0.0), axis=1, keepdims=True) for h in hs]
    beta = [jnp.sum(jnp.where(lane == h + GDN_HEADS, gb, 0.0), axis=1, keepdims=True) for h in hs]
    gc_row = [gbt_ref[0, pl.ds(h, 1), :] for h in hs]
    g_last = [x[c - 1:c, :] for x in gc_col]
    decay = [jnp.where(ri >= ci, jnp.exp(jnp.minimum(gc_col[j] - gc_row[j], 0.0)), 0.0) for j in heads]

    kb = [x.astype(BF16) for x in k]
    k_beta = [k[j] * beta[j] for j in heads]
    kk = [_mm_nt(k_beta[j], kb[j]) for j in heads]
    a_mat = [jnp.where(ri > ci, kk[j] * decay[j], 0.0) for j in heads]
    t_inv = _unit_lower_inverse(a_mat, lvl, eye)
    e_gc = [jnp.exp(x) for x in gc_col]
    rhs = [jnp.concatenate([v[j] * beta[j], k_beta[j] * e_gc[j]], axis=-1).astype(BF16) for j in heads]
    sol = [_dot(t_inv[j], rhs[j]) for j in heads]
    attn = [_mm_nt(q[j], kb[j]) * decay[j] for j in heads]
    q_g = [q[j] * e_gc[j] for j in heads]
    k_g = [k[j] * jnp.exp(g_last[j] - gc_col[j]) for j in heads]

    state = [state_ref[j] for j in heads]
    both = [_mm(jnp.concatenate([sol[j][:, GDN_DV:], q_g[j]], axis=0), state[j]) for j in heads]
    v_new = [(sol[j][:, :GDN_DV] - both[j][:c]).astype(BF16) for j in heads]
    o = [both[j][c:] + _mm(attn[j], v_new[j]) for j in heads]
    for j in heads:
        state_ref[j] = state[j] * jnp.exp(g_last[j]) + _mm(k_g[j].T, v_new[j])
    for j in heads:
        y = o[j] * lax.rsqrt(jnp.mean(o[j] * o[j], axis=-1, keepdims=True) + EPS) * gain_ref[...]
        o_ref[0, :, sls[j]] = (y * _silu(z_ref[0, :, sls[j]])).astype(o_ref.dtype)


def _gdn(p_main, conv_w, gb, gbt, norm_gain):
    b, s, _ = p_main.shape
    c = SEQ_CHUNK
    w = GDN_HB * GDN_DK
    groups = GDN_HEADS // GDN_HB
    qb = COL_GDN // w
    kb = qb + groups
    vb = kb + groups
    zb = vb + groups
    rows_per_chunk = c // SUBLANE

    def col(base):
        return pl.BlockSpec((1, c, w), lambda bi, hi, ci: (bi, ci, base + hi))

    def halo(base):
        return pl.BlockSpec((1, SUBLANE, w),
                            lambda bi, hi, ci: (bi, jnp.maximum(ci * rows_per_chunk - 1, 0), base + hi))

    def convw(base):
        return pl.BlockSpec((CONV_W, w), lambda bi, hi, ci: (0, base + hi))

    lvl = jnp.asarray(_block_levels(c), BF16)
    return pl.pallas_call(
        _gdn_kernel,
        grid=(b, groups, s // c),
        in_specs=[col(qb), col(kb), col(vb), col(zb),
                  halo(qb), halo(kb), halo(vb),
                  convw(0), convw(groups), convw(2 * groups),
                  pl.BlockSpec((1, c, LANE), lambda bi, hi, ci: (bi, ci, 0)),
                  pl.BlockSpec((1, LANE, c), lambda bi, hi, ci: (bi, 0, ci)),
                  pl.BlockSpec((c, c), lambda bi, hi, ci: (0, 0)),
                  pl.BlockSpec((1, GDN_DV), lambda bi, hi, ci: (0, 0))],
        out_specs=pl.BlockSpec((1, c, w), lambda bi, hi, ci: (bi, ci, hi)),
        out_shape=jax.ShapeDtypeStruct((b, s, BRANCH_W), BF16),
        scratch_shapes=[pltpu.VMEM((GDN_HB, GDN_DK, GDN_DV), F32)],
        compiler_params=_cparams(("parallel", "parallel", "arbitrary")),
        name="gdn",
    )(p_main, p_main, p_main, p_main, p_main, p_main, p_main, conv_w, conv_w, conv_w,
      gb, gbt, lvl, norm_gain.reshape(1, GDN_DV))


def _qknorm_kernel(q_ref, k_ref, v_ref, qg_ref, kg_ref, qo_ref, ko_ref, vo_ref):
    def norm(x_ref, g_ref, o_ref, scale):
        g = g_ref[...]
        for i in range(BRANCH_W // DIFF_DK):
            x = x_ref[0, :, i * DIFF_DK:(i + 1) * DIFF_DK]
            y = x * lax.rsqrt(jnp.mean(x * x, axis=-1, keepdims=True) + EPS) * g
            o_ref[0, :, i * DIFF_DK:(i + 1) * DIFF_DK] = (y * scale).astype(o_ref.dtype)

    norm(q_ref, qg_ref, qo_ref, DIFF_DK ** -0.5 * LOG2E)
    norm(k_ref, kg_ref, ko_ref, 1.0)
    vo_ref[...] = v_ref[...].astype(vo_ref.dtype)


def _qknorm(p_main, q_gain, k_gain):
    b, s, _ = p_main.shape
    c = SEQ_CHUNK
    base = COL_DIFF // BRANCH_W

    def col(j):
        return pl.BlockSpec((1, c, BRANCH_W), lambda bi, ci: (bi, ci, base + j))

    out = pl.BlockSpec((1, c, BRANCH_W), lambda bi, ci: (bi, ci, 0))
    gain = pl.BlockSpec((1, DIFF_DK), lambda bi, ci: (0, 0))
    shape = jax.ShapeDtypeStruct((b, s, BRANCH_W), BF16)
    return pl.pallas_call(
        _qknorm_kernel,
        grid=(b, s // c),
        in_specs=[col(0), col(1), col(2), gain, gain],
        out_specs=[out, out, out],
        out_shape=[shape, shape, shape],
        compiler_params=_cparams(("parallel", "parallel")),
        name="qknorm",
    )(p_main, p_main, p_main, q_gain.reshape(1, DIFF_DK), k_gain.reshape(1, DIFF_DK))


def _rel_bucket_np(rel):
    nb = NUM_BUCKETS // 2
    max_exact = nb // 2
    n = np.abs(rel)
    nf = np.maximum(n, 1).astype(np.float64)
    large = max_exact + (np.log(nf / max_exact) / math.log(MAX_DISTANCE / max_exact)
                         * (nb - max_exact)).astype(np.int32)
    large = np.minimum(large, nb - 1)
    return np.where(rel > 0, nb, 0) + np.where(n < max_exact, n, large)


FAR_BUCKET = NUM_BUCKETS // 2 - 1


def _bias_patterns():
    qpos = np.arange(ATT_TQ)[:, None]
    pats = []
    for j in range(ATT_NEAR):
        kpos = np.arange(ATT_TK)[None, :] + (j - 1) * ATT_TK
        bucket = _rel_bucket_np(kpos - qpos)
        visible = (kpos // MASK_CHUNK) <= (qpos // MASK_CHUNK)
        pats.append(np.where(visible, bucket, -1))
    return np.stack(pats).astype(np.int32)


def _bias_tile_kernel(rb_ref, pat_ref, o_ref):
    h = pl.program_id(0)
    pat = pat_ref[0]
    far = rb_ref[FAR_BUCKET, h]
    acc = jnp.full(pat.shape, NEG_MASK, F32)
    for bkt in range(NUM_BUCKETS):
        acc = jnp.where(pat == bkt, (rb_ref[bkt, h] - far) * LOG2E, acc)
    o_ref[0, 0] = acc


def _bias_tiles(rel_bias):
    pats = _bias_patterns()
    return pl.pallas_call(
        _bias_tile_kernel,
        grid=(DIFF_HEADS, ATT_NEAR),
        in_specs=[pl.BlockSpec(memory_space=pltpu.SMEM),
                  pl.BlockSpec((1, ATT_TQ, ATT_TK), lambda hi, pi: (pi, 0, 0))],
        out_specs=pl.BlockSpec((1, 1, ATT_TQ, ATT_TK), lambda hi, pi: (hi, pi, 0, 0)),
        out_shape=jax.ShapeDtypeStruct((DIFF_HEADS, ATT_NEAR, ATT_TQ, ATT_TK), F32),
        compiler_params=_cparams(("parallel", "parallel")),
        name="bias_tiles",
    )(rel_bias, jnp.asarray(pats))


def _diff_attn_kernel(q_ref, k_ref, v_ref, bias_ref, z_ref, lq1_ref, lk1_ref, lq2_ref, lk2_ref,
                      gain_ref, o_ref, s_ref, acc_ref, red_ref, *, lam_init):
    tq, tk = ATT_TQ, ATT_TK
    r = tq // tk
    qi = pl.program_id(2)
    q = q_ref[0]
    qs = (q[:, :DIFF_DK], q[:, DIFF_DK:])

    def key_rows(c0, n):
        return pl.ds(pl.multiple_of(c0 * tk, tk), n * tk)

    def logits(c0, n):
        kt = k_ref[0, key_rows(c0, n), :]
        return [_mm_nt(qs[m], kt[:, m * DIFF_DK:(m + 1) * DIFF_DK]) for m in range(2)]

    def fold_max(mx, s):
        for j in range(s.shape[1] // LANE):
            mx = jnp.maximum(mx, s[:, j * LANE:(j + 1) * LANE])
        return mx

    def far_body(n, base):
        def body(i, carry):
            c0 = base + i * n
            for m, s in enumerate(logits(c0, n)):
                for j in range(n):
                    s_ref[c0 + j, m] = s[:, j * tk:(j + 1) * tk]
                red_ref[m] = fold_max(red_ref[m], s)
            return carry
        return body

    def near(c0, j0):
        for m, s in enumerate(logits(c0, ATT_NEAR - j0)):
            acc = red_ref[m]
            for j in range(ATT_NEAR - j0):
                sj = s[:, j * tk:(j + 1) * tk] + bias_ref[0, j0 + j]
                s_ref[c0 + j, m] = sj
                acc = fold_max(acc, sj)
            red_ref[m] = acc

    def sweep(n_tiles, make_body):
        base = 0
        for width in ATT_UNROLLS:
            trips = (n_tiles - base) // width
            lax.fori_loop(0, trips, make_body(width, base), 0)
            base = base + trips * width

    n_far = jnp.maximum(r * qi - 1, 0)
    red_ref[...] = jnp.full_like(red_ref, NEG_MASK)
    sweep(n_far, far_body)

    @pl.when(qi >= 1)
    def _():
        near(n_far, 0)

    @pl.when(qi == 0)
    def _():
        near(0, 1)

    mrep = [jnp.broadcast_to(jnp.max(red_ref[m], axis=-1, keepdims=True), (tq, LANE)) for m in range(2)]

    acc_ref[...] = jnp.zeros_like(acc_ref)
    red_ref[...] = jnp.zeros_like(red_ref)

    def pv_body(n, base):
        def body(i, carry):
            c0 = base + i * n
            vt = v_ref[0, key_rows(c0, n), :]
            for m in range(2):
                ps = []
                acc = red_ref[m]
                for j in range(n):
                    s = s_ref[c0 + j, m]
                    for jj in range(tk // LANE):
                        p = jnp.exp2(s[:, jj * LANE:(jj + 1) * LANE] - mrep[m])
                        acc = acc + p
                        ps.append(p.astype(BF16))
                red_ref[m] = acc
                acc_ref[m] += jnp.dot(jnp.concatenate(ps, axis=-1), vt, preferred_element_type=F32)
            return carry
        return body

    sweep(r * (qi + 1), pv_body)
    l1, l2 = [jnp.sum(red_ref[m], axis=-1, keepdims=True) for m in range(2)]

    lam = (jnp.exp(jnp.sum(lq1_ref[...] * lk1_ref[...], axis=-1, keepdims=True))
           - jnp.exp(jnp.sum(lq2_ref[...] * lk2_ref[...], axis=-1, keepdims=True)) + lam_init)
    o = acc_ref[0] / l1 - lam * (acc_ref[1] / l2)
    o = o * lax.rsqrt(jnp.mean(o * o, axis=-1, keepdims=True) + EPS) * gain_ref[...]
    o = o * (1.0 - lam_init)
    o_ref[0] = (o * _silu(z_ref[0])).astype(o_ref.dtype)


def _diff_attention(qn, kn, vb, p_main, bias_tiles, lq1, lk1, lq2, lk2, subln_gain, lam_init):
    b, s, _ = qn.shape
    tq, tk = ATT_TQ, ATT_TK
    hw = 2 * DIFF_DK
    zb = (COL_DIFF + 3 * BRANCH_W) // DIFF_DV
    vec = pl.BlockSpec((1, DIFF_DK), lambda bi, hi, qi: (0, 0))
    r = lambda x: x.reshape(1, -1)
    return pl.pallas_call(
        functools.partial(_diff_attn_kernel, lam_init=lam_init),
        grid=(b, DIFF_HEADS, s // tq),
        in_specs=[pl.BlockSpec((1, tq, hw), lambda bi, hi, qi: (bi, qi, hi)),
                  pl.BlockSpec((1, s, hw), lambda bi, hi, qi: (bi, 0, hi)),
                  pl.BlockSpec((1, s, DIFF_DV), lambda bi, hi, qi: (bi, 0, hi)),
                  pl.BlockSpec((1, ATT_NEAR, tq, tk), lambda bi, hi, qi: (hi, 0, 0, 0)),
                  pl.BlockSpec((1, tq, DIFF_DV), lambda bi, hi, qi: (bi, qi, zb + hi)),
                  vec, vec, vec, vec,
                  pl.BlockSpec((1, DIFF_DV), lambda bi, hi, qi: (0, 0))],
        out_specs=pl.BlockSpec((1, tq, DIFF_DV), lambda bi, hi, qi: (bi, qi, hi)),
        out_shape=jax.ShapeDtypeStruct((b, s, BRANCH_W), BF16),
        scratch_shapes=[pltpu.VMEM((s // tk, 2, tq, tk), F32), pltpu.VMEM((2, tq, DIFF_DV), F32),
                        pltpu.VMEM((2, tq, LANE), F32)],
        compiler_params=_cparams(("parallel", "parallel", "arbitrary")),
        name="diff_attention",
    )(qn, kn, vb, bias_tiles, p_main, r(lq1), r(lk1), r(lq2), r(lk2), r(subln_gain))


def _merge_kernel(y0_ref, y1_ref, y2_ref, w0_ref, w1_ref, w2_ref, g0_ref, g1_ref, g2_ref, o_ref):
    acc = None
    for y_ref, w_ref, g_ref in ((y0_ref, w0_ref, g0_ref), (y1_ref, w1_ref, g1_ref), (y2_ref, w2_ref, g2_ref)):
        term = jax.nn.sigmoid(g_ref[...]) * _dot(y_ref[...], w_ref[...])
        acc = term if acc is None else acc + term
    o_ref[...] = acc.astype(o_ref.dtype)


def _merge(ys, w_branch, layer, p_main2d):
    t = ys[0].shape[0]
    tm, tn = 256, 1024
    gate_base = COL_GATE // tn

    def wspec(n):
        return pl.BlockSpec((None, None, BRANCH_W, tn), lambda j, i: (layer, n, 0, j))

    def gspec(n):
        return pl.BlockSpec((tm, tn), lambda j, i: (i, gate_base + n * (D_MODEL // tn) + j))

    yspec = pl.BlockSpec((tm, BRANCH_W), lambda j, i: (i, 0))
    return pl.pallas_call(
        _merge_kernel,
        grid=(D_MODEL // tn, t // tm),
        in_specs=[yspec, yspec, yspec, wspec(0), wspec(1), wspec(2), gspec(0), gspec(1), gspec(2)],
        out_specs=pl.BlockSpec((tm, tn), lambda j, i: (i, j)),
        out_shape=jax.ShapeDtypeStruct((t, D_MODEL), BF16),
        compiler_params=_cparams(("parallel", "arbitrary")),
        name="merge",
    )(*ys, w_branch, w_branch, w_branch, p_main2d, p_main2d, p_main2d)


def _out_proj_kernel(a_ref, w_ref, x_ref, o_ref):
    o_ref[...] = x_ref[...] + _dot(a_ref[...], w_ref[...])


def _out_proj(merged, w_out, layer, x2d):
    t = merged.shape[0]
    tm, tn = 512, 1024
    return pl.pallas_call(
        _out_proj_kernel,
        grid=(D_MODEL // tn, t // tm),
        in_specs=[pl.BlockSpec((tm, D_MODEL), lambda j, i: (i, 0)),
                  pl.BlockSpec((None, D_MODEL, tn), lambda j, i: (layer, 0, j)),
                  pl.BlockSpec((tm, tn), lambda j, i: (i, j))],
        out_specs=pl.BlockSpec((tm, tn), lambda j, i: (i, j)),
        out_shape=jax.ShapeDtypeStruct((t, D_MODEL), F32),
        compiler_params=_cparams(("parallel", "arbitrary")),
        name="out_proj",
    )(merged, w_out, x2d)


def _rotary_tables(s):
    half = RET_DK // 2
    inv = ROPE_BASE ** (-np.arange(half, dtype=np.float64) / half)
    ang = np.arange(s, dtype=np.float64)[:, None] * inv[None, :]
    return jnp.asarray(np.cos(ang), F32), jnp.asarray(np.sin(ang), F32)


def _layer(x, layer, w_in, w2, w_branch, w_out, norm_gain, ret_gn_gain, gdn_conv_w, gdn_a_log,
           gdn_dt_bias, gdn_norm_gain, diff_q_gain, diff_k_gain, lq1, lk1, lq2, lk2, diff_subln_gain,
           bias_tiles, log_gamma, cos, sin):
    b, s, d = x.shape
    t = b * s
    x2d = x.reshape(t, d)

    h = _rmsnorm(x2d, norm_gain)
    p1_2d = _matmul(h, w_in, layer, 0, P1_COLS, F32, 512, 1024, "in_proj_1")
    p2_2d = _matmul(h, w2, layer, 0, P2_COLS, F32, 512, 1024, "in_proj_2")
    p_ab = _matmul(h, w_in, layer, AB_START // LANE, LANE, F32, 512, LANE, "in_proj_ab")
    p1 = p1_2d.reshape(b, s, P1_COLS)
    p2 = p2_2d.reshape(b, s, P2_COLS)

    y_ret = _retention(p1, ret_gn_gain, log_gamma, cos, sin)
    gb, gbt = _gdn_gates(p_ab.reshape(b, s, LANE), gdn_a_log, gdn_dt_bias)
    y_gdn = _gdn(p1, gdn_conv_w, gb, gbt, gdn_norm_gain)
    qn, kn, vb = _qknorm(p2, diff_q_gain, diff_k_gain)
    lam_init = 0.8 - 0.6 * math.exp(-0.3 * layer)
    y_diff = _diff_attention(qn, kn, vb, p2, bias_tiles, lq1, lk1, lq2, lk2,
                             diff_subln_gain, lam_init)

    ys = [y.reshape(t, BRANCH_W) for y in (y_ret, y_gdn, y_diff)]
    merged = _merge(ys, w_branch, layer, p2_2d)
    return _out_proj(merged, w_out, layer, x2d).reshape(b, s, d)


def kernel(x, norm_gain, w_in, ret_gn_gain, gdn_conv_w, gdn_a_log, gdn_dt_bias, gdn_norm_gain,
           diff_q_gain, diff_k_gain, diff_lambda_q1, diff_lambda_k1, diff_lambda_q2, diff_lambda_k2,
           diff_subln_gain, rel_bias, w_branch, w_out):
    depth = w_in.shape[0]
    s = x.shape[1]
    log_gamma = jnp.asarray(np.log(1.0 - 2.0 ** (-5.0 - np.arange(RET_HEADS, dtype=np.float64))), F32)
    cos, sin = _rotary_tables(s)
    bias_tiles = _bias_tiles(rel_bias)
    w_in_b = w_in.astype(BF16)
    w2 = w_in_b[:, :, AB_START + AB_COLS:]
    w_branch_b = w_branch.astype(BF16)
    w_out_b = w_out.astype(BF16)
    for l in range(depth):
        x = _layer(x, l, w_in_b, w2, w_branch_b, w_out_b, norm_gain[l], ret_gn_gain[l], gdn_conv_w[l],
                   gdn_a_log[l], gdn_dt_bias[l], gdn_norm_gain[l], diff_q_gain[l], diff_k_gain[l],
                   diff_lambda_q1[l], diff_lambda_k1[l], diff_lambda_q2[l], diff_lambda_k2[l],
                   diff_subln_gain[l], bias_tiles, log_gamma, cos, sin)
    return x
```

```python
import functools
import math

import numpy as np
import jax
import jax.numpy as jnp
from jax import lax
from jax.experimental import pallas as pl
from jax.experimental.pallas import tpu as pltpu

F32 = jnp.float32
BF16 = jnp.bfloat16

D_MODEL = 4096
EPS = 1e-6
BRANCH_W = 2048
N_BRANCH = 3

RET_HEADS = 8
RET_DK = 256
RET_DV = BRANCH_W // RET_HEADS
ROPE_BASE = 10000.0

GDN_HEADS = 16
GDN_DK = 128
GDN_DV = BRANCH_W // GDN_HEADS
CONV_W = 4

DIFF_HEADS = 8
DIFF_DK = 128
DIFF_DV = BRANCH_W // DIFF_HEADS
MASK_CHUNK = 64
NUM_BUCKETS = 32
MAX_DISTANCE = 128

AB_START = 16384
AB_COLS = 2 * GDN_HEADS
P1_COLS = 16384
P2_COLS = 20480
COL_RET = 0
COL_GDN = 8192
COL_DIFF = 0
COL_GATE = 8192

LANE = 128
SUBLANE = 8
SEQ_CHUNK = 256
NEG_MASK = -1e30
LOG2E = math.log2(math.e)
RET_HB = 4
GDN_HB = 4
ATT_TQ = 512
ATT_TK = 256
ATT_UNROLLS = (4, 2, 1)
ATT_NEAR = ATT_TQ // ATT_TK + 1
VMEM_LIMIT = 56 * 1024 * 1024


def _cparams(sem):
    return pltpu.CompilerParams(dimension_semantics=sem, vmem_limit_bytes=VMEM_LIMIT)


def _dot(a, b):
    return jnp.dot(a, b, preferred_element_type=F32)


def _mm(a, b):
    return _dot(a.astype(BF16), b.astype(BF16))


def _mm_nt(a, b):
    return lax.dot_general(a.astype(BF16), b.astype(BF16), (((1,), (1,)), ((), ())),
                           preferred_element_type=F32)


def _silu(x):
    return x * jax.nn.sigmoid(x)


def _rmsnorm_kernel(x_ref, g_ref, o_ref):
    x = x_ref[...]
    ms = jnp.mean(x * x, axis=-1, keepdims=True)
    o_ref[...] = (x * lax.rsqrt(ms + EPS) * g_ref[...]).astype(o_ref.dtype)


def _rmsnorm(x2d, gain):
    t, d = x2d.shape
    tm = 256
    return pl.pallas_call(
        _rmsnorm_kernel,
        grid=(t // tm,),
        in_specs=[pl.BlockSpec((tm, d), lambda i: (i, 0)),
                  pl.BlockSpec((1, d), lambda i: (0, 0))],
        out_specs=pl.BlockSpec((tm, d), lambda i: (i, 0)),
        out_shape=jax.ShapeDtypeStruct((t, d), BF16),
        compiler_params=_cparams(("parallel",)),
        name="rmsnorm",
    )(x2d, gain.reshape(1, d))


def _matmul_kernel(a_ref, w_ref, o_ref):
    o_ref[...] = _dot(a_ref[...], w_ref[...].astype(BF16)).astype(o_ref.dtype)


def _matmul(a, w, layer, col_block, n, out_dtype, tm, tn, name):
    m, k = a.shape
    return pl.pallas_call(
        _matmul_kernel,
        grid=(n // tn, m // tm),
        in_specs=[pl.BlockSpec((tm, k), lambda j, i: (i, 0)),
                  pl.BlockSpec((None, k, tn), lambda j, i: (layer, 0, col_block + j))],
        out_specs=pl.BlockSpec((tm, tn), lambda j, i: (i, j)),
        out_shape=jax.ShapeDtypeStruct((m, n), out_dtype),
        compiler_params=_cparams(("parallel", "arbitrary")),
        name=name,
    )(a, w)


def _retention_kernel(lg_ref, q_ref, k_ref, v_ref, z_ref, cos_ref, sin_ref, gain_ref,
                      o_ref, state_ref):
    c = SEQ_CHUNK

    @pl.when(pl.program_id(2) == 0)
    def _():
        state_ref[...] = jnp.zeros_like(state_ref)

    cos = cos_ref[...]
    sin = sin_ref[...]
    half = RET_DK // 2

    def rot(x):
        x1 = x[:, :half]
        x2 = x[:, half:]
        return jnp.concatenate([x1 * cos - x2 * sin, x1 * sin + x2 * cos], axis=-1)

    heads = range(RET_HB)
    sls = [slice(j * RET_DK, (j + 1) * RET_DK) for j in heads]
    lg = [lg_ref[pl.program_id(1) * RET_HB + j] for j in heads]
    ri = lax.broadcasted_iota(jnp.int32, (c, c), 0)
    ci = lax.broadcasted_iota(jnp.int32, (c, c), 1)
    rel = (ri - ci).astype(F32)
    rel_pos = jnp.maximum(rel, 0.0)
    idx = lax.broadcasted_iota(jnp.int32, (c, 1), 0).astype(F32)

    q = [rot(q_ref[0, :, sl]).astype(BF16) for sl in sls]
    k = [rot(k_ref[0, :, sl]) * (RET_DK ** -0.5) for sl in sls]
    v = [v_ref[0, :, sl].astype(BF16) for sl in sls]
    dmat = [jnp.where(rel >= 0, jnp.exp(rel_pos * lg[j]), 0.0) for j in heads]
    q_decay = [jnp.exp((idx + 1.0) * lg[j]) for j in heads]
    k_decay = [jnp.exp((c - 1.0 - idx) * lg[j]) for j in heads]
    chunk_decay = [jnp.exp(jnp.full((1, 1), float(c), F32) * lg[j]) for j in heads]

    scores = [_mm_nt(q[j], k[j]) * dmat[j] for j in heads]
    state = [state_ref[j] for j in heads]
    inter = [_mm(q[j], state[j]) * q_decay[j] for j in heads]
    intra = [_mm(scores[j], v[j]) for j in heads]
    for j in heads:
        state_ref[j] = state[j] * chunk_decay[j] + _mm((k[j] * k_decay[j]).T, v[j])
    for j in heads:
        o = intra[j] + inter[j]
        mu = jnp.mean(o, axis=-1, keepdims=True)
        var = jnp.mean(jnp.square(o - mu), axis=-1, keepdims=True)
        o = (o - mu) * lax.rsqrt(var + EPS) * gain_ref[:, sls[j]]
        o_ref[0, :, sls[j]] = (o * _silu(z_ref[0, :, sls[j]])).astype(o_ref.dtype)


def _retention(p_main, gn_gain, log_gamma, cos, sin):
    b, s, _ = p_main.shape
    c = SEQ_CHUNK
    w = RET_HB * RET_DK
    groups = RET_HEADS // RET_HB
    qb = COL_RET // w
    kb = qb + groups
    vb = kb + groups
    zb = vb + groups

    def col(base):
        return pl.BlockSpec((1, c, w), lambda bi, hi, ci: (bi, ci, base + hi))

    return pl.pallas_call(
        _retention_kernel,
        grid=(b, groups, s // c),
        in_specs=[pl.BlockSpec(memory_space=pltpu.SMEM),
                  col(qb), col(kb), col(vb), col(zb),
                  pl.BlockSpec((c, RET_DK // 2), lambda bi, hi, ci: (ci, 0)),
                  pl.BlockSpec((c, RET_DK // 2), lambda bi, hi, ci: (ci, 0)),
                  pl.BlockSpec((1, w), lambda bi, hi, ci: (0, hi))],
        out_specs=pl.BlockSpec((1, c, w), lambda bi, hi, ci: (bi, ci, hi)),
        out_shape=jax.ShapeDtypeStruct((b, s, BRANCH_W), BF16),
        scratch_shapes=[pltpu.VMEM((RET_HB, RET_DK, RET_DV), F32)],
        compiler_params=_cparams(("parallel", "parallel", "arbitrary")),
        name="retention",
    )(log_gamma, p_main, p_main, p_main, p_main, cos, sin, gn_gain.reshape(1, BRANCH_W))


def _gdn_gate_kernel(ab_ref, alog_ref, dt_ref, gb_ref, gbt_ref):
    c = SEQ_CHUNK
    x = ab_ref[0]
    sp = jnp.maximum(x + dt_ref[...], 0.0) + jnp.log1p(jnp.exp(-jnp.abs(x + dt_ref[...])))
    g = -jnp.exp(alog_ref[...]) * sp
    ri = lax.broadcasted_iota(jnp.int32, (c, c), 0)
    ci = lax.broadcasted_iota(jnp.int32, (c, c), 1)
    tri = jnp.where(ri >= ci, 1.0, 0.0).astype(F32)
    gc = jnp.dot(tri, g, preferred_element_type=F32, precision=lax.Precision.HIGHEST)
    lane = lax.broadcasted_iota(jnp.int32, x.shape, 1)
    out = jnp.where(lane < GDN_HEADS, gc, jax.nn.sigmoid(x))
    gb_ref[0] = out
    gbt_ref[0] = out.T


def _gdn_gates(p_ab, a_log, dt_bias):
    b, s, _ = p_ab.shape
    c = SEQ_CHUNK
    pad = LANE - GDN_HEADS
    alog = jnp.pad(a_log, (0, pad)).reshape(1, LANE)
    dt = jnp.pad(dt_bias, (0, pad)).reshape(1, LANE)
    return pl.pallas_call(
        _gdn_gate_kernel,
        grid=(b, s // c),
        in_specs=[pl.BlockSpec((1, c, LANE), lambda bi, ci: (bi, ci, 0)),
                  pl.BlockSpec((1, LANE), lambda bi, ci: (0, 0)),
                  pl.BlockSpec((1, LANE), lambda bi, ci: (0, 0))],
        out_specs=[pl.BlockSpec((1, c, LANE), lambda bi, ci: (bi, ci, 0)),
                   pl.BlockSpec((1, LANE, c), lambda bi, ci: (bi, 0, ci))],
        out_shape=[jax.ShapeDtypeStruct((b, s, LANE), F32),
                   jax.ShapeDtypeStruct((b, LANE, s), F32)],
        compiler_params=_cparams(("parallel", "parallel")),
        name="gdn_gates",
    )(p_ab, alog, dt)


def _block_levels(c):
    idx = np.arange(c)
    x = idx[:, None] ^ idx[None, :]
    return np.where(x > 0, np.floor(np.log2(np.maximum(x, 1))) + 1, 0).astype(np.float32)


def _unit_lower_inverse(a, lvl, eye):
    n = range(len(a))
    ab = [x.astype(BF16) for x in a]
    zero = jnp.zeros_like(ab[0])
    base_shift = 3
    a8 = [jnp.where(lvl <= base_shift, ab[i], zero) for i in n]
    t = [eye - a8[i].astype(F32) for i in n]
    a2 = [_dot(a8[i], a8[i]) for i in n]
    t = [t[i] + _mm(t[i], a2[i]) for i in n]
    a4 = [_mm(a2[i], a2[i]) for i in n]
    tb = [(t[i] + _mm(t[i], a4[i])).astype(BF16) for i in n]
    shift = base_shift
    while (1 << shift) < a[0].shape[0]:
        off = [jnp.where(lvl == shift + 1, ab[i], zero) for i in n]
        x = [_dot(tb[i], off[i]).astype(BF16) for i in n]
        tb = [tb[i] - _dot(x[i], tb[i]).astype(BF16) for i in n]
        shift += 1
    return tb


def _gdn_kernel(q_ref, k_ref, v_ref, z_ref, qh_ref, kh_ref, vh_ref, wq_ref, wk_ref, wv_ref,
                gb_ref, gbt_ref, lvl_ref, gain_ref, o_ref, state_ref):
    c = SEQ_CHUNK
    first = pl.program_id(2) == 0

    @pl.when(first)
    def _():
        state_ref[...] = jnp.zeros_like(state_ref)

    gb = gb_ref[0]
    lane = lax.broadcasted_iota(jnp.int32, gb.shape, 1)
    ri = lax.broadcasted_iota(jnp.int32, (c, c), 0)
    ci = lax.broadcasted_iota(jnp.int32, (c, c), 1)
    eye = jnp.where(ri == ci, 1.0, 0.0).astype(F32)
    lvl = lvl_ref[...]

    def conv_silu(x_ref, halo_ref, w_ref, sl):
        halo = jnp.where(first, 0.0, halo_ref[0, :, sl])
        xx = jnp.concatenate([halo, x_ref[0, :, sl]], axis=0)
        w = w_ref[:, sl]
        y = xx[SUBLANE:SUBLANE + c] * w[CONV_W - 1:CONV_W]
        for i in range(CONV_W - 1):
            off = SUBLANE - (CONV_W - 1) + i
            y = y + xx[off:off + c] * w[i:i + 1]
        return _silu(y)

    heads = range(GDN_HB)
    sls = [slice(j * GDN_DK, (j + 1) * GDN_DK) for j in heads]
    hs = [pl.program_id(1) * GDN_HB + j for j in heads]
    q = [conv_silu(q_ref, qh_ref, wq_ref, sl) for sl in sls]
    k = [conv_silu(k_ref, kh_ref, wk_ref, sl) for sl in sls]
    v = [conv_silu(v_ref, vh_ref, wv_ref, sl) for sl in sls]
    q = [x * lax.rsqrt(jnp.sum(x * x, axis=-1, keepdims=True) + EPS) * (GDN_DK ** -0.5) for x in q]
    k = [x * lax.rsqrt(jnp.sum(x * x, axis=-1, keepdims=True) + EPS) for x in k]

    gc_col = [jnp.sum(jnp.where(lane == h, gb, 0.0), axis=1, keepdims=True) for h in hs]
    beta = [jnp.sum(jnp.where(lane == h + GDN_HEADS, gb, 0.0), axis=1, keepdims=True) for h in hs]
    gc_row = [gbt_ref[0, pl.ds(h, 1), :] for h in hs]
    g_last = [x[c - 1:c, :] for x in gc_col]
    decay = [jnp.where(ri >= ci, jnp.exp(jnp.minimum(gc_col[j] - gc_row[j], 0.0)), 0.0) for j in heads]

    kb = [x.astype(BF16) for x in k]
    k_beta = [k[j] * beta[j] for j in heads]
    kk = [_mm_nt(k_beta[j], kb[j]) for j in heads]
    a_mat = [jnp.where(ri > ci, kk[j] * decay[j], 0.0) for j in heads]
    t_inv = _unit_lower_inverse(a_mat, lvl, eye)
    e_gc = [jnp.exp(x) for x in gc_col]
    rhs = [jnp.concatenate([v[j] * beta[j], k_beta[j] * e_gc[j]], axis=-1).astype(BF16) for j in heads]
    sol = [_dot(t_inv[j], rhs[j]) for j in heads]
    attn = [_mm_nt(q[j], kb[j]) * decay[j] for j in heads]
    q_g = [q[j] * e_gc[j] for j in heads]
    k_g = [k[j] * jnp.exp(g_last[j] - gc_col[j]) for j in heads]

    state = [state_ref[j] for j in heads]
    both = [_mm(jnp.concatenate([sol[j][:, GDN_DV:], q_g[j]], axis=0), state[j]) for j in heads]
    v_new = [(sol[j][:, :GDN_DV] - both[j][:c]).astype(BF16) for j in heads]
    o = [both[j][c:] + _mm(attn[j], v_new[j]) for j in heads]
    for j in heads:
        state_ref[j] = state[j] * jnp.exp(g_last[j]) + _mm(k_g[j].T, v_new[j])
    for j in heads:
        y = o[j] * lax.rsqrt(jnp.mean(o[j] * o[j], axis=-1, keepdims=True) + EPS) * gain_ref[...]
        o_ref[0, :, sls[j]] = (y * _silu(z_ref[0, :, sls[j]])).astype(o_ref.dtype)


def _gdn(p_main, conv_w, gb, gbt, norm_gain):
    b, s, _ = p_main.shape
    c = SEQ_CHUNK
    w = GDN_HB * GDN_DK
    groups = GDN_HEADS // GDN_HB
    qb = COL_GDN // w
    kb = qb + groups
    vb = kb + groups
    zb = vb + groups
    rows_per_chunk = c // SUBLANE

    def col(base):
        return pl.BlockSpec((1, c, w), lambda bi, hi, ci: (bi, ci, base + hi))

    def halo(base):
        return pl.BlockSpec((1, SUBLANE, w),
                            lambda bi, hi, ci: (bi, jnp.maximum(ci * rows_per_chunk - 1, 0), base + hi))

    def convw(base):
        return pl.BlockSpec((CONV_W, w), lambda bi, hi, ci: (0, base + hi))

    lvl = jnp.asarray(_block_levels(c), BF16)
    return pl.pallas_call(
        _gdn_kernel,
        grid=(b, groups, s // c),
        in_specs=[col(qb), col(kb), col(vb), col(zb),
                  halo(qb), halo(kb), halo(vb),
                  convw(0), convw(groups), convw(2 * groups),
                  pl.BlockSpec((1, c, LANE), lambda bi, hi, ci: (bi, ci, 0)),
                  pl.BlockSpec((1, LANE, c), lambda bi, hi, ci: (bi, 0, ci)),
                  pl.BlockSpec((c, c), lambda bi, hi, ci: (0, 0)),
                  pl.BlockSpec((1, GDN_DV), lambda bi, hi, ci: (0, 0))],
        out_specs=pl.BlockSpec((1, c, w), lambda bi, hi, ci: (bi, ci, hi)),
        out_shape=jax.ShapeDtypeStruct((b, s, BRANCH_W), BF16),
        scratch_shapes=[pltpu.VMEM((GDN_HB, GDN_DK, GDN_DV), F32)],
        compiler_params=_cparams(("parallel", "parallel", "arbitrary")),
        name="gdn",
    )(p_main, p_main, p_main, p_main, p_main, p_main, p_main, conv_w, conv_w, conv_w,
      gb, gbt, lvl, norm_gain.reshape(1, GDN_DV))


def _qknorm_kernel(q_ref, k_ref, v_ref, qg_ref, kg_ref, qo_ref, ko_ref, vo_ref):
    def norm(x_ref, g_ref, o_ref, scale):
        g = g_ref[...]
        for i in range(BRANCH_W // DIFF_DK):
            x = x_ref[0, :, i * DIFF_DK:(i + 1) * DIFF_DK]
            y = x * lax.rsqrt(jnp.mean(x * x, axis=-1, keepdims=True) + EPS) * g
            o_ref[0, :, i * DIFF_DK:(i + 1) * DIFF_DK] = (y * scale).astype(o_ref.dtype)

    norm(q_ref, qg_ref, qo_ref, DIFF_DK ** -0.5 * LOG2E)
    norm(k_ref, kg_ref, ko_ref, 1.0)
    vo_ref[...] = v_ref[...].astype(vo_ref.dtype)


def _qknorm(p_main, q_gain, k_gain):
    b, s, _ = p_main.shape
    c = SEQ_CHUNK
    base = COL_DIFF // BRANCH_W

    def col(j):
        return pl.BlockSpec((1, c, BRANCH_W), lambda bi, ci: (bi, ci, base + j))

    out = pl.BlockSpec((1, c, BRANCH_W), lambda bi, ci: (bi, ci, 0))
    gain = pl.BlockSpec((1, DIFF_DK), lambda bi, ci: (0, 0))
    shape = jax.ShapeDtypeStruct((b, s, BRANCH_W), BF16)
    return pl.pallas_call(
        _qknorm_kernel,
        grid=(b, s // c),
        in_specs=[col(0), col(1), col(2), gain, gain],
        out_specs=[out, out, out],
        out_shape=[shape, shape, shape],
        compiler_params=_cparams(("parallel", "parallel")),
        name="qknorm",
    )(p_main, p_main, p_main, q_gain.reshape(1, DIFF_DK), k_gain.reshape(1, DIFF_DK))


def _rel_bucket_np(rel):
    nb = NUM_BUCKETS // 2
    max_exact = nb // 2
    n = np.abs(rel)
    nf = np.maximum(n, 1).astype(np.float64)
    large = max_exact + (np.log(nf / max_exact) / math.log(MAX_DISTANCE / max_exact)
                         * (nb - max_exact)).astype(np.int32)
    large = np.minimum(large, nb - 1)
    return np.where(rel > 0, nb, 0) + np.where(n < max_exact, n, large)


FAR_BUCKET = NUM_BUCKETS // 2 - 1


def _bias_patterns():
    qpos = np.arange(ATT_TQ)[:, None]
    pats = []
    for j in range(ATT_NEAR):
        kpos = np.arange(ATT_TK)[None, :] + (j - 1) * ATT_TK
        bucket = _rel_bucket_np(kpos - qpos)
        visible = (kpos // MASK_CHUNK) <= (qpos // MASK_CHUNK)
        pats.append(np.where(visible, bucket, -1))
    return np.stack(pats).astype(np.int32)


def _bias_tile_kernel(rb_ref, pat_ref, o_ref):
    h = pl.program_id(0)
    pat = pat_ref[0]
    far = rb_ref[FAR_BUCKET, h]
    acc = jnp.full(pat.shape, NEG_MASK, F32)
    for bkt in range(NUM_BUCKETS):
        acc = jnp.where(pat == bkt, (rb_ref[bkt, h] - far) * LOG2E, acc)
    o_ref[0, 0] = acc


def _bias_tiles(rel_bias):
    pats = _bias_patterns()
    return pl.pallas_call(
        _bias_tile_kernel,
        grid=(DIFF_HEADS, ATT_NEAR),
        in_specs=[pl.BlockSpec(memory_space=pltpu.SMEM),
                  pl.BlockSpec((1, ATT_TQ, ATT_TK), lambda hi, pi: (pi, 0, 0))],
        out_specs=pl.BlockSpec((1, 1, ATT_TQ, ATT_TK), lambda hi, pi: (hi, pi, 0, 0)),
        out_shape=jax.ShapeDtypeStruct((DIFF_HEADS, ATT_NEAR, ATT_TQ, ATT_TK), F32),
        compiler_params=_cparams(("parallel", "parallel")),
        name="bias_tiles",
    )(rel_bias, jnp.asarray(pats))


def _diff_attn_kernel(q_ref, k_ref, v_ref, bias_ref, z_ref, lq1_ref, lk1_ref, lq2_ref, lk2_ref,
                      gain_ref, o_ref, s_ref, acc_ref, red_ref, *, lam_init):
    tq, tk = ATT_TQ, ATT_TK
    r = tq // tk
    qi = pl.program_id(2)
    q = q_ref[0]
    qs = (q[:, :DIFF_DK], q[:, DIFF_DK:])

    def key_rows(c0, n):
        return pl.ds(pl.multiple_of(c0 * tk, tk), n * tk)

    def logits(c0, n):
        kt = k_ref[0, key_rows(c0, n), :]
        return [_mm_nt(qs[m], kt[:, m * DIFF_DK:(m + 1) * DIFF_DK]) for m in range(2)]

    def fold_max(mx, s):
        for j in range(s.shape[1] // LANE):
            mx = jnp.maximum(mx, s[:, j * LANE:(j + 1) * LANE])
        return mx

    def far_body(n, base):
        def body(i, carry):
            c0 = base + i * n
            for m, s in enumerate(logits(c0, n)):
                for j in range(n):
                    s_ref[c0 + j, m] = s[:, j * tk:(j + 1) * tk]
                red_ref[m] = fold_max(red_ref[m], s)
            return carry
        return body

    def near(c0, j0):
        for m, s in enumerate(logits(c0, ATT_NEAR - j0)):
            acc = red_ref[m]
            for j in range(ATT_NEAR - j0):
                sj = s[:, j * tk:(j + 1) * tk] + bias_ref[0, j0 + j]
                s_ref[c0 + j, m] = sj
                acc = fold_max(acc, sj)
            red_ref[m] = acc

    def sweep(n_tiles, make_body):
        base = 0
        for width in ATT_UNROLLS:
            trips = (n_tiles - base) // width
            lax.fori_loop(0, trips, make_body(width, base), 0)
            base = base + trips * width

    n_far = jnp.maximum(r * qi - 1, 0)
    red_ref[...] = jnp.full_like(red_ref, NEG_MASK)
    sweep(n_far, far_body)

    @pl.when(qi >= 1)
    def _():
        near(n_far, 0)

    @pl.when(qi == 0)
    def _():
        near(0, 1)

    mrep = [jnp.broadcast_to(jnp.max(red_ref[m], axis=-1, keepdims=True), (tq, LANE)) for m in range(2)]

    acc_ref[...] = jnp.zeros_like(acc_ref)
    red_ref[...] = jnp.zeros_like(red_ref)

    def pv_body(n, base):
        def body(i, carry):
            c0 = base + i * n
            vt = v_ref[0, key_rows(c0, n), :]
            for m in range(2):
                ps = []
                acc = red_ref[m]
                for j in range(n):
                    s = s_ref[c0 + j, m]
                    for jj in range(tk // LANE):
                        p = jnp.exp2(s[:, jj * LANE:(jj + 1) * LANE] - mrep[m])
                        acc = acc + p
                        ps.append(p.astype(BF16))
                red_ref[m] = acc
                acc_ref[m] += jnp.dot(jnp.concatenate(ps, axis=-1), vt, preferred_element_type=F32)
            return carry
        return body

    sweep(r * (qi + 1), pv_body)
    l1, l2 = [jnp.sum(red_ref[m], axis=-1, keepdims=True) for m in range(2)]

    lam = (jnp.exp(jnp.sum(lq1_ref[...] * lk1_ref[...], axis=-1, keepdims=True))
           - jnp.exp(jnp.sum(lq2_ref[...] * lk2_ref[...], axis=-1, keepdims=True)) + lam_init)
    o = acc_ref[0] / l1 - lam * (acc_ref[1] / l2)
    o = o * lax.rsqrt(jnp.mean(o * o, axis=-1, keepdims=True) + EPS) * gain_ref[...]
    o = o * (1.0 - lam_init)
    o_ref[0] = (o * _silu(z_ref[0])).astype(o_ref.dtype)


def _diff_attention(qn, kn, vb, p_main, bias_tiles, lq1, lk1, lq2, lk2, subln_gain, lam_init):
    b, s, _ = qn.shape
    tq, tk = ATT_TQ, ATT_TK
    hw = 2 * DIFF_DK
    zb = (COL_DIFF + 3 * BRANCH_W) // DIFF_DV
    vec = pl.BlockSpec((1, DIFF_DK), lambda bi, hi, qi: (0, 0))
    r = lambda x: x.reshape(1, -1)
    return pl.pallas_call(
        functools.partial(_diff_attn_kernel, lam_init=lam_init),
        grid=(b, DIFF_HEADS, s // tq),
        in_specs=[pl.BlockSpec((1, tq, hw), lambda bi, hi, qi: (bi, qi, hi)),
                  pl.BlockSpec((1, s, hw), lambda bi, hi, qi: (bi, 0, hi)),
                  pl.BlockSpec((1, s, DIFF_DV), lambda bi, hi, qi: (bi, 0, hi)),
                  pl.BlockSpec((1, ATT_NEAR, tq, tk), lambda bi, hi, qi: (hi, 0, 0, 0)),
                  pl.BlockSpec((1, tq, DIFF_DV), lambda bi, hi, qi: (bi, qi, zb + hi)),
                  vec, vec, vec, vec,
                  pl.BlockSpec((1, DIFF_DV), lambda bi, hi, qi: (0, 0))],
        out_specs=pl.BlockSpec((1, tq, DIFF_DV), lambda bi, hi, qi: (bi, qi, hi)),
        out_shape=jax.ShapeDtypeStruct((b, s, BRANCH_W), BF16),
        scratch_shapes=[pltpu.VMEM((s // tk, 2, tq, tk), F32), pltpu.VMEM((2, tq, DIFF_DV), F32),
                        pltpu.VMEM((2, tq, LANE), F32)],
        compiler_params=_cparams(("parallel", "parallel", "arbitrary")),
        name="diff_attention",
    )(qn, kn, vb, bias_tiles, p_main, r(lq1), r(lk1), r(lq2), r(lk2), r(subln_gain))


def _merge_kernel(y0_ref, y1_ref, y2_ref, w0_ref, w1_ref, w2_ref, g0_ref, g1_ref, g2_ref, o_ref):
    acc = None
    for y_ref, w_ref, g_ref in ((y0_ref, w0_ref, g0_ref), (y1_ref, w1_ref, g1_ref), (y2_ref, w2_ref, g2_ref)):
        term = jax.nn.sigmoid(g_ref[...]) * _dot(y_ref[...], w_ref[...])
        acc = term if acc is None else acc + term
    o_ref[...] = acc.astype(o_ref.dtype)


def _merge(ys, w_branch, layer, p_main2d):
    t = ys[0].shape[0]
    tm, tn = 256, 1024
    gate_base = COL_GATE // tn

    def wspec(n):
        return pl.BlockSpec((None, None, BRANCH_W, tn), lambda j, i: (layer, n, 0, j))

    def gspec(n):
        return pl.BlockSpec((tm, tn), lambda j, i: (i, gate_base + n * (D_MODEL // tn) + j))

    yspec = pl.BlockSpec((tm, BRANCH_W), lambda j, i: (i, 0))
    return pl.pallas_call(
        _merge_kernel,
        grid=(D_MODEL // tn, t // tm),
        in_specs=[yspec, yspec, yspec, wspec(0), wspec(1), wspec(2), gspec(0), gspec(1), gspec(2)],
        out_specs=pl.BlockSpec((tm, tn), lambda j, i: (i, j)),
        out_shape=jax.ShapeDtypeStruct((t, D_MODEL), BF16),
        compiler_params=_cparams(("parallel", "arbitrary")),
        name="merge",
    )(*ys, w_branch, w_branch, w_branch, p_main2d, p_main2d, p_main2d)


def _out_proj_kernel(a_ref, w_ref, x_ref, o_ref):
    o_ref[...] = x_ref[...] + _dot(a_ref[...], w_ref[...])


def _out_proj(merged, w_out, layer, x2d):
    t = merged.shape[0]
    tm, tn = 512, 1024
    return pl.pallas_call(
        _out_proj_kernel,
        grid=(D_MODEL // tn, t // tm),
        in_specs=[pl.BlockSpec((tm, D_MODEL), lambda j, i: (i, 0)),
                  pl.BlockSpec((None, D_MODEL, tn), lambda j, i: (layer, 0, j)),
                  pl.BlockSpec((tm, tn), lambda j, i: (i, j))],
        out_specs=pl.BlockSpec((tm, tn), lambda j, i: (i, j)),
        out_shape=jax.ShapeDtypeStruct((t, D_MODEL), F32),
        compiler_params=_cparams(("parallel", "arbitrary")),
        name="out_proj",
    )(merged, w_out, x2d)


def _rotary_tables(s):
    half = RET_DK // 2
    inv = ROPE_BASE ** (-np.arange(half, dtype=np.float64) / half)
    ang = np.arange(s, dtype=np.float64)[:, None] * inv[None, :]
    return jnp.asarray(np.cos(ang), F32), jnp.asarray(np.sin(ang), F32)


def _layer(x, layer, w_in, w2, w_branch, w_out, norm_gain, ret_gn_gain, gdn_conv_w, gdn_a_log,
           gdn_dt_bias, gdn_norm_gain, diff_q_gain, diff_k_gain, lq1, lk1, lq2, lk2, diff_subln_gain,
           bias_tiles, log_gamma, cos, sin):
    b, s, d = x.shape
    t = b * s
    x2d = x.reshape(t, d)

    h = _rmsnorm(x2d, norm_gain)
    p1_2d = _matmul(h, w_in, layer, 0, P1_COLS, F32, 512, 1024, "in_proj_1")
    p2_2d = _matmul(h, w2, layer, 0, P2_COLS, F32, 512, 1024, "in_proj_2")
    p_ab = _matmul(h, w_in, layer, AB_START // LANE, LANE, F32, 512, LANE, "in_proj_ab")
    p1 = p1_2d.reshape(b, s, P1_COLS)
    p2 = p2_2d.reshape(b, s, P2_COLS)

    y_ret = _retention(p1, ret_gn_gain, log_gamma, cos, sin)
    gb, gbt = _gdn_gates(p_ab.reshape(b, s, LANE), gdn_a_log, gdn_dt_bias)
    y_gdn = _gdn(p1, gdn_conv_w, gb, gbt, gdn_norm_gain)
    qn, kn, vb = _qknorm(p2, diff_q_gain, diff_k_gain)
    lam_init = 0.8 - 0.6 * math.exp(-0.3 * layer)
    y_diff = _diff_attention(qn, kn, vb, p2, bias_tiles, lq1, lk1, lq2, lk2,
                             diff_subln_gain, lam_init)

    ys = [y.reshape(t, BRANCH_W) for y in (y_ret, y_gdn, y_diff)]
    merged = _merge(ys, w_branch, layer, p2_2d)
    return _out_proj(merged, w_out, layer, x2d).reshape(b, s, d)


def kernel(x, norm_gain, w_in, ret_gn_gain, gdn_conv_w, gdn_a_log, gdn_dt_bias, gdn_norm_gain,
           diff_q_gain, diff_k_gain, diff_lambda_q1, diff_lambda_k1, diff_lambda_q2, diff_lambda_k2,
           diff_subln_gain, rel_bias, w_branch, w_out):
    depth = w_in.shape[0]
    s = x.shape[1]
    log_gamma = jnp.asarray(np.log(1.0 - 2.0 ** (-5.0 - np.arange(RET_HEADS, dtype=np.float64))), F32)
    cos, sin = _rotary_tables(s)
    bias_tiles = _bias_tiles(rel_bias)
    w_in_b = w_in.astype(BF16)
    w2 = w_in_b[:, :, AB_START + AB_COLS:]
    w_branch_b = w_branch.astype(BF16)
    w_out_b = w_out.astype(BF16)
    for l in range(depth):
        x = _layer(x, l, w_in_b, w2, w_branch_b, w_out_b, norm_gain[l], ret_gn_gain[l], gdn_conv_w[l],
                   gdn_a_log[l], gdn_dt_bias[l], gdn_norm_gain[l], diff_q_gain[l], diff_k_gain[l],
                   diff_lambda_q1[l], diff_lambda_k1[l], diff_lambda_q2[l], diff_lambda_k2[l],
                   diff_subln_gain[l], bias_tiles, log_gamma, cos, sin)
    return x
```
